```python
import math
import jax
import jax.numpy as jnp
from jax import lax
import numpy as np

D_MODEL = 4096
BATCH = 4
SEQ = 2048
DEPTH = 1
DEC_BATCH = 128
DEC_SEQ = 8
PAST_LEN = 2048
PAGE_SIZE = 128

MIX_WIDTH = D_MODEL
GDN_DK = 128
GDN_DV = 128
GDN_HEADS = MIX_WIDTH // 2 // GDN_DV
GDN_CONV = 4
GDN_CHUNK = 64
GDN_QK = GDN_HEADS * GDN_DK
GDN_VW = GDN_HEADS * GDN_DV
GDN_CONV_DIM = 2 * GDN_QK + GDN_VW
NSA_DIM = 128
NSA_HEADS = (MIX_WIDTH - GDN_VW) // NSA_DIM
NSA_GROUP = 4
NSA_KV_HEADS = NSA_HEADS // NSA_GROUP
NSA_QW = NSA_HEADS * NSA_DIM
NSA_KVW = NSA_KV_HEADS * NSA_DIM
CMP_STRIDE = 16
CMP_LEN = 2 * CMP_STRIDE
CMP_HIDDEN = NSA_DIM
SLC_BLOCK = 64
SLC_TOPK = 16
WINDOW = 512
WIN_QBLOCK = 128
SEL_CHUNK = 32
ROT_DIM = NSA_DIM // 4
ROPE_THETA = 500000.0
PEER_HEADS = 8
PEER_KEYS = 128
PEER_EXPERTS = PEER_KEYS * PEER_KEYS
PEER_TOPK = 16
PEER_QDIM = 256
PEER_CHUNK = 64
DN_ALPHA = (2.0 * DEPTH) ** 0.25
DN_BETA = (8.0 * DEPTH) ** -0.25
LN_EPS = 1e-5
NEG = -1e30
FORCE = 1e30
IN_SIZES = (GDN_QK, GDN_QK, GDN_VW, GDN_VW, GDN_HEADS, GDN_HEADS, NSA_QW, NSA_KVW, NSA_KVW, NSA_KVW, NSA_KVW, NSA_KVW, NSA_KVW, 3 * NSA_HEADS)

kernel_name = 'hymba_gdn_nsa_peer_deepnorm_step'


def layer_norm(x, g, b):
    xf = x.astype(jnp.float32)
    mu = jnp.mean(xf, axis=-1, keepdims=True)
    var = jnp.mean(jnp.square(xf - mu), axis=-1, keepdims=True)
    return ((xf - mu) * lax.rsqrt(var + LN_EPS) * g + b).astype(x.dtype)


def rms_norm(x, g):
    xf = x.astype(jnp.float32)
    return xf * lax.rsqrt(jnp.mean(xf * xf, axis=-1, keepdims=True) + 1e-6) * g


def l2_normalize(x):
    return x * lax.rsqrt(jnp.sum(x * x, axis=-1, keepdims=True) + 1e-6)


def partial_rope(x, pos):
    half = ROT_DIM // 2
    inv = ROPE_THETA ** (-jnp.arange(half, dtype=jnp.float32) / half)
    ang = pos.astype(jnp.float32)[:, None] * inv[None, :]
    cos = jnp.cos(ang)[None, :, None, :]
    sin = jnp.sin(ang)[None, :, None, :]
    xr = x[..., :ROT_DIM].astype(jnp.float32)
    x1, x2 = xr[..., :half], xr[..., half:]
    rot = jnp.concatenate([x1 * cos - x2 * sin, x2 * cos + x1 * sin], axis=-1).astype(x.dtype)
    return jnp.concatenate([rot, x[..., ROT_DIM:]], axis=-1)


def mixing_projection(x, w_in):
    cuts = np.cumsum(IN_SIZES)[:-1].tolist()
    return jnp.split(jnp.einsum('btd,dc->btc', x, w_in), cuts, axis=-1)


def causal_short_conv(x, buf, w):
    T = x.shape[1]
    xp = jnp.concatenate([buf, x], axis=1)
    y = w[0] * xp[:, 0:T]
    for j in range(1, GDN_CONV):
        y = y + w[j] * xp[:, j:j + T]
    return jax.nn.silu(y), xp[:, xp.shape[1] - (GDN_CONV - 1):]


def gated_delta_rule(q, k, v, g, beta, s0):
    B, T, H, dk = q.shape
    dv = v.shape[-1]
    C = T if T <= GDN_CHUNK else math.gcd(T, GDN_CHUNK)
    n = T // C
    q = l2_normalize(q) * (dk ** -0.5)
    k = l2_normalize(k)

    def chunks(a):
        a = a.reshape((B, n, C, H) + a.shape[3:])
        return jnp.moveaxis(a, (1, 3), (0, 2))

    qc, kc, vc, bc = chunks(q), chunks(k), chunks(v), chunks(beta)
    gc = jnp.cumsum(chunks(g), axis=-1)
    incl = jnp.tril(jnp.ones((C, C), bool))
    strict = jnp.tril(jnp.ones((C, C), bool), -1)
    decay = jnp.exp(jnp.where(incl, gc[..., :, None] - gc[..., None, :], -jnp.inf))
    kb = kc * bc[..., None]
    vb = vc * bc[..., None]
    a = jnp.where(strict, jnp.einsum('nbhid,nbhjd->nbhij', kb, kc) * decay, 0.0)
    eye = jnp.eye(C, dtype=a.dtype)
    t_inv = lax.linalg.triangular_solve(eye + a, jnp.broadcast_to(eye, a.shape), left_side=True, lower=True)
    u = t_inv @ vb
    w = t_inv @ (kb * jnp.exp(gc)[..., None])
    qk = jnp.where(incl, jnp.einsum('nbhid,nbhjd->nbhij', qc, kc) * decay, 0.0)
    q_dec = qc * jnp.exp(gc)[..., None]
    k_dec = kc * jnp.exp(gc[..., -1:] - gc)[..., None]
    g_last = jnp.exp(gc[..., -1])

    def step(s, xs):
        qk_i, u_i, w_i, q_i, k_i, gl = xs
        v_new = u_i - w_i @ s
        o = q_i @ s + qk_i @ v_new
        s = s * gl[..., None, None] + jnp.swapaxes(k_i, -1, -2) @ v_new
        return s, o

    s_final, o = lax.scan(step, s0, (qk, u, w, q_dec, k_dec, g_last))
    o = jnp.moveaxis(o, (0, 2), (1, 3)).reshape(B, T, H, dv)
    return o, s_final


def gdn_mixer(q, k, v, z, a, b, conv_buf, s0, conv_w, a_log, dt_bias, norm_w):
    B, T, _ = q.shape
    f32 = jnp.float32
    qkv, conv_new = causal_short_conv(jnp.concatenate([q, k, v], axis=-1), conv_buf, conv_w)
    q, k, v = jnp.split(qkv.astype(f32), [GDN_QK, 2 * GDN_QK], axis=-1)
    g = -jnp.exp(a_log.astype(f32)) * jax.nn.softplus(a.astype(f32) + dt_bias.astype(f32))
    beta = jax.nn.sigmoid(b.astype(f32))
    o, s_new = gated_delta_rule(q.reshape(B, T, GDN_HEADS, GDN_DK), k.reshape(B, T, GDN_HEADS, GDN_DK),
                                v.reshape(B, T, GDN_HEADS, GDN_DV), g, beta, s0.astype(f32))
    o = rms_norm(o, norm_w) * jax.nn.silu(z.astype(f32).reshape(B, T, GDN_HEADS, GDN_DV))
    return o.reshape(B, T, GDN_VW).astype(z.dtype), conv_new, s_new.astype(s0.dtype)


def compress_blocks(x, pos_emb, w1, w2):
    B, Tk, G, d = x.shape
    nsub = Tk // CMP_STRIDE
    sub = x[:, :nsub * CMP_STRIDE].reshape(B, nsub, CMP_STRIDE, G, d)
    blocks = jnp.concatenate([sub[:, :-1], sub[:, 1:]], axis=2) + pos_emb[:, None, :]
    h = jax.nn.gelu(jnp.einsum('bnjgd,jde->bnge', blocks, w1.reshape(CMP_LEN, d, CMP_HIDDEN)))
    return jnp.einsum('bnge,ef->bngf', h, w2)


def selection_matrix(n_cmp, n_slc):
    r = SLC_BLOCK // CMP_STRIDE
    s = CMP_LEN // CMP_STRIDE
    wts = np.convolve(np.ones(r), np.ones(s)).astype(np.float32)
    off = np.arange(n_cmp)[:, None] - r * np.arange(n_slc)[None, :]
    m = np.where((off >= 0) & (off < wts.size), wts[np.clip(off, 0, wts.size - 1)], 0.0)
    return jnp.asarray(m, jnp.float32)


def selected_attention(q, k, v, idx, ok, q_pos):
    B, T, H, d = q.shape
    Tk = k.shape[1]
    G = NSA_KV_HEADS
    R = H // G
    K = idx.shape[-1]
    N = B * T
    C = math.gcd(N, SEL_CHUNK)
    kf = k.reshape(B * Tk * G, d)
    vf = v.reshape(B * Tk * G, d)
    qf = q.reshape(N // C, C, G, R, d)
    idxf = jnp.swapaxes(idx, 1, 2).reshape(N // C, C, G, K)
    okf = jnp.swapaxes(ok, 1, 2).reshape(N // C, C, G, K)
    base = jnp.repeat(jnp.arange(B, dtype=jnp.int32) * Tk, T).reshape(N // C, C)
    posf = jnp.tile(q_pos, B).reshape(N // C, C)
    offs = jnp.arange(SLC_BLOCK, dtype=jnp.int32)
    gid = jnp.arange(G, dtype=jnp.int32)

    def one(xs):
        qc, ic, okc, bc, pc = xs
        tok = (ic[..., None] * SLC_BLOCK + offs).reshape(C, G, K * SLC_BLOCK)
        keep = jnp.broadcast_to(okc[..., None], (C, G, K, SLC_BLOCK)).reshape(C, G, K * SLC_BLOCK) & (tok <= pc[:, None, None])
        rows = (jnp.minimum(tok, Tk - 1) + bc[:, None, None]) * G + gid[None, :, None]
        kg = kf[rows]
        vg = vf[rows]
        s = jnp.einsum('cgrd,cgsd->cgrs', qc, kg).astype(jnp.float32) * (d ** -0.5)
        p = jax.nn.softmax(jnp.where(keep[:, :, None, :], s, NEG), axis=-1)
        return jnp.einsum('cgrs,cgsd->cgrd', p.astype(vg.dtype), vg)

    o = lax.map(one, (qf, idxf, okf, base, posf))
    return o.reshape(B, T, H, d)


def nsa_compressed_selected(q, q_rot, kc_all, vc_all, ks_all, vs_all, q_pos, pos_k, w1_k, w2_k, pos_v, w1_v, w2_v):
    B, T, H, d = q.shape
    Tk = kc_all.shape[1]
    G = NSA_KV_HEADS
    R = H // G
    k_cmp = compress_blocks(kc_all, pos_k, w1_k, w2_k)
    v_cmp = compress_blocks(vc_all, pos_v, w1_v, w2_v)
    n_cmp = k_cmp.shape[1]
    s = jnp.einsum('btgrd,bngd->bgrtn', q.reshape(B, T, G, R, d), k_cmp).astype(jnp.float32) * (d ** -0.5)
    blk_end = jnp.arange(n_cmp, dtype=jnp.int32) * CMP_STRIDE + (CMP_LEN - 1)
    cmask = blk_end[None, :] <= q_pos[:, None]
    p = jnp.where(cmask, jax.nn.softmax(jnp.where(cmask, s, NEG), axis=-1), 0.0)
    o_cmp = jnp.einsum('bgrtn,bngd->btgrd', p.astype(v_cmp.dtype), v_cmp).reshape(B, T, H, d)
    n_slc = -(-Tk // SLC_BLOCK)
    imp = jnp.einsum('bgrtn,nj->bgtj', p, selection_matrix(n_cmp, n_slc))
    blk = jnp.arange(n_slc, dtype=jnp.int32)
    cur = q_pos // SLC_BLOCK
    valid = blk[None, :] * SLC_BLOCK <= q_pos[:, None]
    forced = (blk[None, :] == 0) | (blk[None, :] == cur[:, None]) | (blk[None, :] == cur[:, None] - 1)
    score = jnp.where(valid, jnp.where(forced, FORCE, imp), NEG)
    top_s, top_i = lax.top_k(score, min(SLC_TOPK, n_slc))
    o_slc = selected_attention(q_rot, ks_all, vs_all, top_i, top_s > 0.5 * NEG, q_pos)
    return o_cmp, o_slc


def window_attention_banded(q, k, v):
    B, T, H, d = q.shape
    G = NSA_KV_HEADS
    R = H // G
    QB = WIN_QBLOCK
    NB = WINDOW // QB
    NQ = T // QB
    KB = (NB + 1) * QB
    pad = jnp.zeros((B, WINDOW, G, d), k.dtype)
    kb = jnp.concatenate([pad, k], axis=1).reshape(B, NB + NQ, QB, G, d)
    vb = jnp.concatenate([pad, v], axis=1).reshape(B, NB + NQ, QB, G, d)
    k_band = jnp.concatenate([kb[:, i:i + NQ] for i in range(NB + 1)], axis=2)
    v_band = jnp.concatenate([vb[:, i:i + NQ] for i in range(NB + 1)], axis=2)
    q_pos = jnp.arange(NQ)[:, None] * QB + jnp.arange(QB)[None, :]
    k_pos = jnp.arange(NQ)[:, None] * QB - WINDOW + jnp.arange(KB)[None, :]
    kp = k_pos[:, None, :]
    qp = q_pos[:, :, None]
    mask = (kp >= 0) & (kp <= qp) & (kp > qp - WINDOW)
    s = jnp.einsum('bnqgrd,bnkgd->bngrqk', q.reshape(B, NQ, QB, G, R, d), k_band).astype(jnp.float32) * (d ** -0.5)
    p = jax.nn.softmax(jnp.where(mask[None, :, None, None], s, NEG), axis=-1)
    o = jnp.einsum('bngrqk,bnkgd->bnqgrd', p.astype(v.dtype), v_band)
    return o.reshape(B, T, H, d)


def window_attention_cached(q, k_new, v_new, k_buf, v_buf, q_pos):
    B, T, H, d = q.shape
    G = NSA_KV_HEADS
    R = H // G
    Wb = k_buf.shape[1]
    k = jnp.concatenate([k_buf, k_new], axis=1)
    v = jnp.concatenate([v_buf, v_new], axis=1)
    k_pos = jnp.concatenate([PAST_LEN - Wb + jnp.arange(Wb, dtype=jnp.int32), q_pos])
    mask = (k_pos[None, :] <= q_pos[:, None]) & (k_pos[None, :] > q_pos[:, None] - WINDOW)
    s = jnp.einsum('btgrd,bsgd->bgrts', q.reshape(B, T, G, R, d), k).astype(jnp.float32) * (d ** -0.5)
    p = jax.nn.softmax(jnp.where(mask, s, NEG), axis=-1)
    o = jnp.einsum('bgrts,bsgd->btgrd', p.astype(v.dtype), v).reshape(B, T, H, d)
    return o, k[:, k.shape[1] - Wb:], v[:, v.shape[1] - Wb:]


def peer_ffn(x, w_query, sub_keys, expert_u, expert_v):
    B, T, D = x.shape
    N = B * T
    xf = x.reshape(N, D)
    q = jnp.einsum('nd,dc->nc', xf, w_query).reshape(N, PEER_HEADS, 2, PEER_QDIM // 2)
    s = jnp.einsum('nhpc,hpkc->nhpk', q, sub_keys).astype(jnp.float32)
    s1, i1 = lax.top_k(s[:, :, 0], PEER_TOPK)
    s2, i2 = lax.top_k(s[:, :, 1], PEER_TOPK)
    cand_s = (s1[..., :, None] + s2[..., None, :]).reshape(N, PEER_HEADS, PEER_TOPK * PEER_TOPK)
    cand_i = (i1[..., :, None] * PEER_KEYS + i2[..., None, :]).reshape(N, PEER_HEADS, PEER_TOPK * PEER_TOPK)
    top_s, pos = lax.top_k(cand_s, PEER_TOPK)
    idx = jnp.take_along_axis(cand_i, pos, axis=-1)
    gate = jax.nn.softmax(top_s, axis=-1)
    C = math.gcd(N, PEER_CHUNK)

    def one(xs):
        xc, ic, gc = xs
        h = jax.nn.gelu(jnp.einsum('chkd,cd->chk', expert_u[ic], xc).astype(jnp.float32))
        return jnp.einsum('chk,chkd->cd', (gc * h).astype(xc.dtype), expert_v[ic])

    y = lax.map(one, (xf.reshape(N // C, C, D), idx.reshape(N // C, C, PEER_HEADS, PEER_TOPK),
                      gate.reshape(N // C, C, PEER_HEADS, PEER_TOPK)))
    return y.reshape(B, T, D)


def trunk_layer(x, pos, past_kc, past_vc, past_ks, past_vs, win_k_buf, win_v_buf, conv_buf, ssm0, weights):
    (w_in, conv_w, a_log, dt_bias, norm_w, pos_k, w1_k, w2_k, pos_v, w1_v, w2_v, w_out,
     ln1_g, ln1_b, peer_wq, peer_keys, peer_u, peer_v, ln2_g, ln2_b) = weights
    B, T, _ = x.shape
    H, G, hd = NSA_HEADS, NSA_KV_HEADS, NSA_DIM
    (gq, gk, gv, gz, ga, gb, nq, kc, vc, ks, vs, kw, vw, ngate) = mixing_projection(x, w_in)
    o_gdn, conv_new, ssm_new = gdn_mixer(gq, gk, gv, gz, ga, gb, conv_buf, ssm0, conv_w, a_log, dt_bias, norm_w)
    q = nq.reshape(B, T, H, hd)
    q_rot = partial_rope(q, pos)
    kc = kc.reshape(B, T, G, hd)
    vc = vc.reshape(B, T, G, hd)
    ks = partial_rope(ks.reshape(B, T, G, hd), pos)
    vs = vs.reshape(B, T, G, hd)
    kw = partial_rope(kw.reshape(B, T, G, hd), pos)
    vw = vw.reshape(B, T, G, hd)
    o_cmp, o_slc = nsa_compressed_selected(
        q, q_rot, jnp.concatenate([past_kc, kc], axis=1), jnp.concatenate([past_vc, vc], axis=1),
        jnp.concatenate([past_ks, ks], axis=1), jnp.concatenate([past_vs, vs], axis=1),
        pos, pos_k, w1_k, w2_k, pos_v, w1_v, w2_v)
    if win_k_buf is None:
        o_win = window_attention_banded(q_rot, kw, vw)
        keep = min(WINDOW, T)
        win_k_new, win_v_new = kw[:, T - keep:], vw[:, T - keep:]
    else:
        o_win, win_k_new, win_v_new = window_attention_cached(q_rot, kw, vw, win_k_buf, win_v_buf, pos)
    gate = jax.nn.sigmoid(ngate.astype(jnp.float32)).reshape(B, T, 3, H, 1).astype(x.dtype)
    o_nsa = gate[:, :, 0] * o_cmp + gate[:, :, 1] * o_slc + gate[:, :, 2] * o_win
    mixed = jnp.einsum('btc,cd->btd', jnp.concatenate([o_gdn, o_nsa.reshape(B, T, NSA_QW)], axis=-1), w_out)
    x = layer_norm(DN_ALPHA * x + mixed, ln1_g, ln1_b)
    x = layer_norm(DN_ALPHA * x + peer_ffn(x, peer_wq, peer_keys, peer_u, peer_v), ln2_g, ln2_b)
    return x, (kc, vc, ks, vs, win_k_new, win_v_new, conv_new, ssm_new)


def setup_inputs(seed: int = 0) -> dict:
    key = jax.random.key(seed)
    keys = iter(jax.random.split(key, 48))
    f32 = jnp.float32

    def rnd(shape, scale):
        return jax.random.normal(next(keys), shape, f32) * scale

    G, hd = NSA_KV_HEADS, NSA_DIM
    n_pages = PAST_LEN // PAGE_SIZE
    n_used = DEC_BATCH * n_pages
    n_phys = (5 * n_used + 3) // 4
    w_buf = min(WINDOW, PAST_LEN)
    pool = (DEPTH, n_phys, PAGE_SIZE, G, hd)
    s_in = D_MODEL ** -0.5
    col_scale = (s_in, s_in, s_in * DN_BETA, s_in, s_in, s_in, s_in,
                 s_in, s_in * DN_BETA, s_in, s_in * DN_BETA, s_in, s_in * DN_BETA, s_in)
    w_in = jnp.concatenate([rnd((DEPTH, D_MODEL, n), sc) for n, sc in zip(IN_SIZES, col_scale)], axis=-1)
    page_table = jax.random.permutation(next(keys), n_phys)[:n_used].reshape(DEC_BATCH, n_pages).astype(jnp.int32)
    dt = jnp.exp(jax.random.uniform(next(keys), (DEPTH, GDN_HEADS), f32, math.log(1e-3), math.log(1e-1)))
    a_log = jnp.log(jax.random.uniform(next(keys), (DEPTH, GDN_HEADS), f32, 1.0, 16.0))
    return {
        'x_prompt': rnd((BATCH, SEQ, D_MODEL), 1.0),
        'x_sample': rnd((DEC_BATCH, DEC_SEQ, D_MODEL), 1.0),
        'cache_cmp_k': rnd(pool, 1.0),
        'cache_cmp_v': rnd(pool, 1.0),
        'cache_slc_k': rnd(pool, 1.0),
        'cache_slc_v': rnd(pool, 1.0),
        'cache_win_k': rnd((DEPTH, DEC_BATCH, w_buf, G, hd), 1.0),
        'cache_win_v': rnd((DEPTH, DEC_BATCH, w_buf, G, hd), 1.0),
        'state_conv': rnd((DEPTH, DEC_BATCH, GDN_CONV - 1, GDN_CONV_DIM), 1.0),
        'state_ssm': rnd((DEPTH, DEC_BATCH, GDN_HEADS, GDN_DK, GDN_DV), 0.1),
        'page_table': page_table,
        'w_in': w_in,
        'gdn_conv_w': rnd((DEPTH, GDN_CONV, GDN_CONV_DIM), 0.5),
        'gdn_a_log': a_log,
        'gdn_dt_bias': dt + jnp.log(-jnp.expm1(-dt)),
        'gdn_norm_w': 1.0 + rnd((DEPTH, GDN_DV), 0.05),
        'cmp_pos_k': rnd((DEPTH, CMP_LEN, hd), 0.1),
        'cmp_w1_k': rnd((DEPTH, CMP_LEN * hd, CMP_HIDDEN), (CMP_LEN * hd) ** -0.5),
        'cmp_w2_k': rnd((DEPTH, CMP_HIDDEN, hd), CMP_HIDDEN ** -0.5),
        'cmp_pos_v': rnd((DEPTH, CMP_LEN, hd), 0.1),
        'cmp_w1_v': rnd((DEPTH, CMP_LEN * hd, CMP_HIDDEN), (CMP_LEN * hd) ** -0.5),
        'cmp_w2_v': rnd((DEPTH, CMP_HIDDEN, hd), CMP_HIDDEN ** -0.5),
        'w_out': rnd((DEPTH, MIX_WIDTH, D_MODEL), MIX_WIDTH ** -0.5 * DN_BETA),
        'ln1_g': 1.0 + rnd((DEPTH, D_MODEL), 0.05),
        'ln1_b': rnd((DEPTH, D_MODEL), 0.02),
        'peer_w_query': rnd((DEPTH, D_MODEL, PEER_HEADS * PEER_QDIM), s_in),
        'peer_sub_keys': rnd((DEPTH, PEER_HEADS, 2, PEER_KEYS, PEER_QDIM // 2), (PEER_QDIM // 2) ** -0.5),
        'peer_u': rnd((DEPTH, PEER_EXPERTS, D_MODEL), s_in),
        'peer_v': rnd((DEPTH, PEER_EXPERTS, D_MODEL), DN_BETA),
        'ln2_g': 1.0 + rnd((DEPTH, D_MODEL), 0.05),
        'ln2_b': rnd((DEPTH, D_MODEL), 0.02),
    }


def reference(x_prompt, x_sample, cache_cmp_k, cache_cmp_v, cache_slc_k, cache_slc_v, cache_win_k, cache_win_v,
              state_conv, state_ssm, page_table, w_in, gdn_conv_w, gdn_a_log, gdn_dt_bias, gdn_norm_w,
              cmp_pos_k, cmp_w1_k, cmp_w2_k, cmp_pos_v, cmp_w1_v, cmp_w2_v, w_out, ln1_g, ln1_b,
              peer_w_query, peer_sub_keys, peer_u, peer_v, ln2_g, ln2_b):
    G, hd = NSA_KV_HEADS, NSA_DIM
    B, T, _ = x_prompt.shape
    DB, L, _ = x_sample.shape
    pos_p = jnp.arange(T, dtype=jnp.int32)
    pos_s = PAST_LEN + jnp.arange(L, dtype=jnp.int32)
    h_p, h_s = x_prompt, x_sample
    st_p, st_s = [], []
    for l in range(DEPTH):
        weights = (w_in[l], gdn_conv_w[l], gdn_a_log[l], gdn_dt_bias[l], gdn_norm_w[l],
                   cmp_pos_k[l], cmp_w1_k[l], cmp_w2_k[l], cmp_pos_v[l], cmp_w1_v[l], cmp_w2_v[l], w_out[l],
                   ln1_g[l], ln1_b[l], peer_w_query[l], peer_sub_keys[l], peer_u[l], peer_v[l], ln2_g[l], ln2_b[l])
        empty = jnp.zeros((B, 0, G, hd), x_prompt.dtype)
        h_p, new_p = trunk_layer(h_p, pos_p, empty, empty, empty, empty, None, None,
                                 jnp.zeros((B, GDN_CONV - 1, GDN_CONV_DIM), x_prompt.dtype),
                                 jnp.zeros((B, GDN_HEADS, GDN_DK, GDN_DV), state_ssm.dtype), weights)

        def paged(pool):
            return pool[l][page_table].reshape(DB, -1, G, hd)

        h_s, new_s = trunk_layer(h_s, pos_s, paged(cache_cmp_k), paged(cache_cmp_v), paged(cache_slc_k),
                                 paged(cache_slc_v), cache_win_k[l], cache_win_v[l], state_conv[l], state_ssm[l],
                                 weights)
        st_p.append(new_p)
        st_s.append(new_s)
    (p_cmp_k, p_cmp_v, p_slc_k, p_slc_v, p_win_k, p_win_v, p_conv, p_ssm) = [jnp.stack([s[i] for s in st_p]) for i in range(8)]
    (s_cmp_k, s_cmp_v, s_slc_k, s_slc_v, s_win_k, s_win_v, s_conv, s_ssm) = [jnp.stack([s[i] for s in st_s]) for i in range(8)]
    return (h_p, h_s, p_cmp_k, p_cmp_v, p_slc_k, p_slc_v, p_win_k, p_win_v, p_conv, p_ssm,
            s_cmp_k, s_cmp_v, s_slc_k, s_slc_v, s_win_k, s_win_v, s_conv, s_ssm)
```

```python
import functools
import math

import jax
import jax.numpy as jnp
import numpy as np
from jax import lax
from jax.experimental import pallas as pl
from jax.experimental.pallas import tpu as pltpu

D_MODEL = 4096
PAST_LEN = 2048
GDN_DK = 128
GDN_DV = 128
GDN_HEADS = D_MODEL // 2 // GDN_DV
GDN_CONV = 4
GDN_CHUNK = 64
GDN_QK = GDN_HEADS * GDN_DK
GDN_VW = GDN_HEADS * GDN_DV
NSA_DIM = 128
NSA_HEADS = (D_MODEL - GDN_VW) // NSA_DIM
NSA_GROUP = 4
NSA_KV_HEADS = NSA_HEADS // NSA_GROUP
NSA_QW = NSA_HEADS * NSA_DIM
NSA_KVW = NSA_KV_HEADS * NSA_DIM
CMP_STRIDE = 16
CMP_LEN = 2 * CMP_STRIDE
CMP_HIDDEN = NSA_DIM
SLC_BLOCK = 64
SLC_TOPK = 16
WINDOW = 512
ROT_DIM = NSA_DIM // 4
ROPE_THETA = 500000.0
PEER_HEADS = 8
PEER_KEYS = 128
PEER_EXPERTS = PEER_KEYS * PEER_KEYS
PEER_TOPK = 16
PEER_QDIM = 256
DEPTH = 1
DN_ALPHA = (2.0 * DEPTH) ** 0.25
LN_EPS = 1e-5
NEG = -1e30
FORCE = 1e30
IN_SIZES = (GDN_QK, GDN_QK, GDN_VW, GDN_VW, GDN_HEADS, GDN_HEADS, NSA_QW, NSA_KVW, NSA_KVW, NSA_KVW, NSA_KVW,
            NSA_KVW, NSA_KVW, 3 * NSA_HEADS)

LANES = 128
VMEM_LIMIT = 56 * 1024 * 1024

F32 = jnp.float32
BF16 = jnp.bfloat16


def _matmul_kernel(x_ref, w_ref, o_ref):
    o_ref[...] = jnp.dot(x_ref[...], w_ref[...], preferred_element_type=F32)


def matmul(x, w, tm, tn, name):
    m, k = x.shape
    n = w.shape[1]
    assert m % tm == 0 and n % tn == 0
    return pl.pallas_call(
        _matmul_kernel,
        grid=(n // tn, m // tm),
        in_specs=[pl.BlockSpec((tm, k), lambda j, i: (i, 0)), pl.BlockSpec((k, tn), lambda j, i: (0, j))],
        out_specs=pl.BlockSpec((tm, tn), lambda j, i: (i, j)),
        out_shape=jax.ShapeDtypeStruct((m, n), F32),
        compiler_params=pltpu.CompilerParams(dimension_semantics=("parallel", "parallel"),
                                             vmem_limit_bytes=VMEM_LIMIT),
        name=name,
    )(x, w)


def _gelu_tanh(x):
    return 0.5 * x * (1.0 + jnp.tanh(math.sqrt(2.0 / math.pi) * (x + 0.044715 * (x * x * x))))


def _peer_expert_kernel(x_ref, u_ref, v_ref, g_ref, o_ref):
    @pl.when(pl.program_id(1) == 0)
    def _():
        o_ref[...] = jnp.zeros_like(o_ref)

    s = lax.dot_general(x_ref[...], u_ref[...], (((1,), (1,)), ((), ())), preferred_element_type=F32)
    h = (g_ref[...] * _gelu_tanh(s)).astype(BF16)
    o_ref[...] += jnp.dot(h, v_ref[...], preferred_element_type=F32)


def peer_experts(x, u, v, gate, tn, te):
    n, d = x.shape
    e = u.shape[0]
    assert n % tn == 0 and e % te == 0
    return pl.pallas_call(
        _peer_expert_kernel,
        grid=(n // tn, e // te),
        in_specs=[pl.BlockSpec((tn, d), lambda i, j: (i, 0)), pl.BlockSpec((te, d), lambda i, j: (j, 0)),
                  pl.BlockSpec((te, d), lambda i, j: (j, 0)), pl.BlockSpec((tn, te), lambda i, j: (i, j))],
        out_specs=pl.BlockSpec((tn, d), lambda i, j: (i, 0)),
        out_shape=jax.ShapeDtypeStruct((n, d), F32),
        compiler_params=pltpu.CompilerParams(dimension_semantics=("parallel", "arbitrary"),
                                             vmem_limit_bytes=VMEM_LIMIT),
        name="peer_experts",
    )(x, u, v, gate)


def layer_norm(x, g, b):
    mu = jnp.mean(x, axis=-1, keepdims=True)
    var = jnp.mean(jnp.square(x - mu), axis=-1, keepdims=True)
    return (x - mu) * lax.rsqrt(var + LN_EPS) * g + b


def rms_norm(x, g):
    return x * lax.rsqrt(jnp.mean(x * x, axis=-1, keepdims=True) + 1e-6) * g


def l2_normalize(x):
    return x * lax.rsqrt(jnp.sum(x * x, axis=-1, keepdims=True) + 1e-6)


def partial_rope(x, pos):
    half = ROT_DIM // 2
    inv = ROPE_THETA ** (-jnp.arange(half, dtype=F32) / half)
    ang = pos.astype(F32)[:, None] * inv[None, :]
    cos = jnp.cos(ang)[None, :, None, :]
    sin = jnp.sin(ang)[None, :, None, :]
    xr = x[..., :ROT_DIM]
    x1, x2 = xr[..., :half], xr[..., half:]
    rot = jnp.concatenate([x1 * cos - x2 * sin, x2 * cos + x1 * sin], axis=-1)
    return jnp.concatenate([rot, x[..., ROT_DIM:]], axis=-1)


def causal_short_conv(x, buf, w):
    T = x.shape[1]
    xp = jnp.concatenate([buf, x], axis=1)
    y = w[0] * xp[:, 0:T]
    for j in range(1, GDN_CONV):
        y = y + w[j] * xp[:, j:j + T]
    return jax.nn.silu(y), xp[:, xp.shape[1] - (GDN_CONV - 1):]


def gated_delta_rule(q, k, v, g, beta, s0):
    B, T, H, dk = q.shape
    dv = v.shape[-1]
    C = T if T <= GDN_CHUNK else math.gcd(T, GDN_CHUNK)
    n = T // C
    q = l2_normalize(q) * (dk ** -0.5)
    k = l2_normalize(k)

    def chunks(a):
        a = a.reshape((B, n, C, H) + a.shape[3:])
        return jnp.moveaxis(a, (1, 3), (0, 2))

    qc, kc, vc, bc = chunks(q), chunks(k), chunks(v), chunks(beta)
    gc = jnp.cumsum(chunks(g), axis=-1)
    incl = jnp.tril(jnp.ones((C, C), bool))
    strict = jnp.tril(jnp.ones((C, C), bool), -1)
    decay = jnp.exp(jnp.where(incl, gc[..., :, None] - gc[..., None, :], -jnp.inf))
    kb = kc * bc[..., None]
    vb = vc * bc[..., None]
    a = jnp.where(strict, jnp.einsum('nbhid,nbhjd->nbhij', kb, kc) * decay, 0.0)
    eye = jnp.eye(C, dtype=a.dtype)
    t_inv = lax.linalg.triangular_solve(eye + a, jnp.broadcast_to(eye, a.shape), left_side=True, lower=True)
    u = t_inv @ vb
    w = t_inv @ (kb * jnp.exp(gc)[..., None])
    qk = jnp.where(incl, jnp.einsum('nbhid,nbhjd->nbhij', qc, kc) * decay, 0.0)
    q_dec = qc * jnp.exp(gc)[..., None]
    k_dec = kc * jnp.exp(gc[..., -1:] - gc)[..., None]
    g_last = jnp.exp(gc[..., -1])

    def step(s, xs):
        qk_i, u_i, w_i, q_i, k_i, gl = xs
        v_new = u_i - w_i @ s
        o = q_i @ s + qk_i @ v_new
        s = s * gl[..., None, None] + jnp.swapaxes(k_i, -1, -2) @ v_new
        return s, o

    s_final, o = lax.scan(step, s0, (qk, u, w, q_dec, k_dec, g_last))
    o = jnp.moveaxis(o, (0, 2), (1, 3)).reshape(B, T, H, dv)
    return o, s_final


def gdn_mixer(q, k, v, z, a, b, conv_buf, s0, conv_w, a_log, dt_bias, norm_w):
    B, T, _ = q.shape
    qkv, conv_new = causal_short_conv(jnp.concatenate([q, k, v], axis=-1), conv_buf, conv_w)
    q, k, v = jnp.split(qkv, [GDN_QK, 2 * GDN_QK], axis=-1)
    g = -jnp.exp(a_log) * jax.nn.softplus(a + dt_bias)
    beta = jax.nn.sigmoid(b)
    o, s_new = gated_delta_rule(q.reshape(B, T, GDN_HEADS, GDN_DK), k.reshape(B, T, GDN_HEADS, GDN_DK),
                                v.reshape(B, T, GDN_HEADS, GDN_DV), g, beta, s0)
    o = rms_norm(o, norm_w) * jax.nn.silu(z.reshape(B, T, GDN_HEADS, GDN_DV))
    return o.reshape(B, T, GDN_VW), conv_new, s_new


def compress_blocks(x, pos_emb, w1, w2):
    B, Tk, G, d = x.shape
    nsub = Tk // CMP_STRIDE
    sub = x[:, :nsub * CMP_STRIDE].reshape(B, nsub, CMP_STRIDE, G, d)
    blocks = jnp.concatenate([sub[:, :-1], sub[:, 1:]], axis=2) + pos_emb[:, None, :]
    h = jax.nn.gelu(jnp.einsum('bnjgd,jde->bnge', blocks, w1.reshape(CMP_LEN, d, CMP_HIDDEN)))
    return jnp.einsum('bnge,ef->bngf', h, w2)


def selection_matrix(n_cmp, n_slc):
    r = SLC_BLOCK // CMP_STRIDE
    s = CMP_LEN // CMP_STRIDE
    wts = np.convolve(np.ones(r), np.ones(s)).astype(np.float32)
    off = np.arange(n_cmp)[:, None] - r * np.arange(n_slc)[None, :]
    m = np.where((off >= 0) & (off < wts.size), wts[np.clip(off, 0, wts.size - 1)], 0.0)
    return jnp.asarray(m, F32)


def masked_attention(q, k, v, mask):
    d = q.shape[-1]
    s = jnp.einsum('btgrd,bsgd->bgrts', q, k) * (d ** -0.5)
    p = jax.nn.softmax(jnp.where(mask, s, NEG), axis=-1)
    return jnp.einsum('bgrts,bsgd->btgrd', p, v)


def nsa_compressed_selected(q, q_rot, kc_all, vc_all, ks_all, vs_all, q_pos, pos_k, w1_k, w2_k, pos_v, w1_v, w2_v):
    B, T, H, d = q.shape
    Tk = ks_all.shape[1]
    G = NSA_KV_HEADS
    R = H // G
    k_cmp = compress_blocks(kc_all, pos_k, w1_k, w2_k)
    v_cmp = compress_blocks(vc_all, pos_v, w1_v, w2_v)
    n_cmp = k_cmp.shape[1]
    s = jnp.einsum('btgrd,bngd->bgrtn', q.reshape(B, T, G, R, d), k_cmp) * (d ** -0.5)
    blk_end = jnp.arange(n_cmp, dtype=jnp.int32) * CMP_STRIDE + (CMP_LEN - 1)
    cmask = blk_end[None, :] <= q_pos[:, None]
    p = jnp.where(cmask, jax.nn.softmax(jnp.where(cmask, s, NEG), axis=-1), 0.0)
    o_cmp = jnp.einsum('bgrtn,bngd->btgrd', p, v_cmp).reshape(B, T, H, d)
    n_slc = -(-Tk // SLC_BLOCK)
    imp = jnp.einsum('bgrtn,nj->bgtj', p, selection_matrix(n_cmp, n_slc))
    blk = jnp.arange(n_slc, dtype=jnp.int32)
    cur = q_pos // SLC_BLOCK
    valid = blk[None, :] * SLC_BLOCK <= q_pos[:, None]
    forced = (blk[None, :] == 0) | (blk[None, :] == cur[:, None]) | (blk[None, :] == cur[:, None] - 1)
    score = jnp.where(valid, jnp.where(forced, FORCE, imp), NEG)
    si, sj = score[..., :, None], score[..., None, :]
    before = (sj > si) | ((sj == si) & (blk[None, :] < blk[:, None]))
    rank = jnp.sum(before, axis=-1)
    sel = (rank < min(SLC_TOPK, n_slc)) & valid
    key_pos = jnp.arange(Tk, dtype=jnp.int32)
    key_sel = jnp.repeat(sel, SLC_BLOCK, axis=-1)[..., :Tk] & (key_pos[None, :] <= q_pos[:, None])
    o_slc = masked_attention(q_rot.reshape(B, T, G, R, d), ks_all, vs_all, key_sel[:, :, None])
    return o_cmp, o_slc.reshape(B, T, H, d)


def window_attention(q, k, v, q_pos, k_pos):
    B, T, H, d = q.shape
    G = NSA_KV_HEADS
    mask = (k_pos[None, :] <= q_pos[:, None]) & (k_pos[None, :] > q_pos[:, None] - WINDOW)
    return masked_attention(q.reshape(B, T, G, H // G, d), k, v, mask).reshape(B, T, H, d)


def peer_routing(q, sub_keys):
    N = q.shape[0]
    q = q.reshape(N, PEER_HEADS, 2, PEER_QDIM // 2)
    s = jnp.einsum('nhpc,hpkc->nhpk', q, sub_keys)
    s1, i1 = lax.top_k(s[:, :, 0], PEER_TOPK)
    s2, i2 = lax.top_k(s[:, :, 1], PEER_TOPK)
    cand_s = (s1[..., :, None] + s2[..., None, :]).reshape(N, PEER_HEADS, PEER_TOPK * PEER_TOPK)
    top_s, pos = lax.top_k(cand_s, PEER_TOPK)
    row = jnp.take_along_axis(i1, pos // PEER_TOPK, axis=-1).reshape(N, PEER_HEADS * PEER_TOPK)
    col = jnp.take_along_axis(i2, pos % PEER_TOPK, axis=-1).reshape(N, PEER_HEADS * PEER_TOPK)
    gate = jax.nn.softmax(top_s, axis=-1).reshape(N, PEER_HEADS * PEER_TOPK)
    keys = jnp.arange(PEER_KEYS, dtype=jnp.int32)
    row_hot = jnp.where(row[..., None] == keys, gate[..., None], 0.0)
    col_hot = (col[..., None] == keys).astype(F32)
    dense = jnp.einsum('nka,nkb->nab', row_hot, col_hot, precision=lax.Precision.HIGHEST)
    return dense.reshape(N, PEER_EXPERTS)


def _permuted_in_weight(w_in):
    cuts = np.cumsum(IN_SIZES)[:-1].tolist()
    parts = jnp.split(w_in, cuts, axis=-1)
    wide = [parts[i] for i in (0, 1, 2, 3, 6, 7, 8, 9, 10, 11, 12)]
    narrow = [parts[i] for i in (4, 5, 13)]
    n_narrow = sum(p.shape[1] for p in narrow)
    pad = jnp.zeros((w_in.shape[0], LANES - n_narrow), w_in.dtype)
    return jnp.concatenate(wide, axis=-1).astype(BF16), jnp.concatenate(narrow + [pad], axis=-1).astype(BF16)


def _split_projection(wide, narrow, B, T):
    sizes = [IN_SIZES[i] for i in (0, 1, 2, 3, 6, 7, 8, 9, 10, 11, 12)]
    cuts = np.cumsum(sizes)[:-1].tolist()
    gq, gk, gv, gz, nq, kc, vc, ks, vs, kw, vw = [p.reshape(B, T, -1) for p in jnp.split(wide, cuts, axis=-1)]
    ga = narrow[:, :GDN_HEADS].reshape(B, T, -1)
    gb = narrow[:, GDN_HEADS:2 * GDN_HEADS].reshape(B, T, -1)
    ngate = narrow[:, 2 * GDN_HEADS:2 * GDN_HEADS + 3 * NSA_HEADS].reshape(B, T, -1)
    return gq, gk, gv, gz, ga, gb, nq, kc, vc, ks, vs, kw, vw, ngate


def _mixers(proj, B, T, pos, past, win_buf, conv_buf, ssm0, weights):
    (conv_w, a_log, dt_bias, norm_w, pos_k, w1_k, w2_k, pos_v, w1_v, w2_v) = weights
    H, G, hd = NSA_HEADS, NSA_KV_HEADS, NSA_DIM
    (gq, gk, gv, gz, ga, gb, nq, kc, vc, ks, vs, kw, vw, ngate) = proj
    o_gdn, conv_new, ssm_new = gdn_mixer(gq, gk, gv, gz, ga, gb, conv_buf, ssm0, conv_w, a_log, dt_bias, norm_w)
    q = nq.reshape(B, T, H, hd)
    q_rot = partial_rope(q, pos)
    kc = kc.reshape(B, T, G, hd)
    vc = vc.reshape(B, T, G, hd)
    ks = partial_rope(ks.reshape(B, T, G, hd), pos)
    vs = vs.reshape(B, T, G, hd)
    kw = partial_rope(kw.reshape(B, T, G, hd), pos)
    vw = vw.reshape(B, T, G, hd)
    if past is None:
        kc_all, vc_all, ks_all, vs_all = kc, vc, ks, vs
    else:
        kc_all, vc_all, ks_all, vs_all = [jnp.concatenate([p, n], axis=1) for p, n in zip(past, (kc, vc, ks, vs))]
    o_cmp, o_slc = nsa_compressed_selected(q, q_rot, kc_all, vc_all, ks_all, vs_all, pos,
                                           pos_k, w1_k, w2_k, pos_v, w1_v, w2_v)
    if win_buf is None:
        o_win = window_attention(q_rot, kw, vw, pos, pos)
        keep = min(WINDOW, T)
        win_k_new, win_v_new = kw[:, T - keep:], vw[:, T - keep:]
    else:
        k_buf, v_buf = win_buf
        Wb = k_buf.shape[1]
        k_all = jnp.concatenate([k_buf, kw], axis=1)
        v_all = jnp.concatenate([v_buf, vw], axis=1)
        k_pos = jnp.concatenate([PAST_LEN - Wb + jnp.arange(Wb, dtype=jnp.int32), pos])
        o_win = window_attention(q_rot, k_all, v_all, pos, k_pos)
        win_k_new, win_v_new = k_all[:, k_all.shape[1] - Wb:], v_all[:, v_all.shape[1] - Wb:]
    gate = jax.nn.sigmoid(ngate).reshape(B, T, 3, H, 1)
    o_nsa = gate[:, :, 0] * o_cmp + gate[:, :, 1] * o_slc + gate[:, :, 2] * o_win
    mix_in = jnp.concatenate([o_gdn, o_nsa.reshape(B, T, NSA_QW)], axis=-1)
    return mix_in.reshape(B * T, D_MODEL), (kc, vc, ks, vs, win_k_new, win_v_new, conv_new, ssm_new)


def kernel(x_prompt, x_sample, cache_cmp_k, cache_cmp_v, cache_slc_k, cache_slc_v, cache_win_k, cache_win_v,
           state_conv, state_ssm, page_table, w_in, gdn_conv_w, gdn_a_log, gdn_dt_bias, gdn_norm_w,
           cmp_pos_k, cmp_w1_k, cmp_w2_k, cmp_pos_v, cmp_w1_v, cmp_w2_v, w_out, ln1_g, ln1_b,
           peer_w_query, peer_sub_keys, peer_u, peer_v, ln2_g, ln2_b):
    assert w_in.shape[0] == DEPTH == 1
    G, hd = NSA_KV_HEADS, NSA_DIM
    B, T, _ = x_prompt.shape
    DB, L, _ = x_sample.shape
    NP, NS = B * T, DB * L
    pos_p = jnp.arange(T, dtype=jnp.int32)
    pos_s = PAST_LEN + jnp.arange(L, dtype=jnp.int32)
    l = 0
    x_all = jnp.concatenate([x_prompt.reshape(NP, D_MODEL), x_sample.reshape(NS, D_MODEL)], axis=0)

    w_wide, w_narrow = _permuted_in_weight(w_in[l])
    x_bf = x_all.astype(BF16)
    proj_wide = matmul(x_bf, w_wide, 512, 1024, "in_proj_wide")
    proj_narrow = matmul(x_bf, w_narrow, 512, LANES, "in_proj_narrow")

    weights = (gdn_conv_w[l], gdn_a_log[l], gdn_dt_bias[l], gdn_norm_w[l], cmp_pos_k[l], cmp_w1_k[l], cmp_w2_k[l],
               cmp_pos_v[l], cmp_w1_v[l], cmp_w2_v[l])
    mix_p, new_p = _mixers(_split_projection(proj_wide[:NP], proj_narrow[:NP], B, T), B, T, pos_p, None, None,
                           jnp.zeros((B, GDN_CONV - 1, state_conv.shape[-1]), F32),
                           jnp.zeros((B, GDN_HEADS, GDN_DK, GDN_DV), F32), weights)

    def paged(pool):
        return pool[l][page_table].reshape(DB, -1, G, hd)

    mix_s, new_s = _mixers(_split_projection(proj_wide[NP:], proj_narrow[NP:], DB, L), DB, L, pos_s,
                           (paged(cache_cmp_k), paged(cache_cmp_v), paged(cache_slc_k), paged(cache_slc_v)),
                           (cache_win_k[l], cache_win_v[l]), state_conv[l], state_ssm[l], weights)

    mix_in = jnp.concatenate([mix_p, mix_s], axis=0).astype(BF16)
    mixed = matmul(mix_in, w_out[l].astype(BF16), 512, 1024, "out_proj")
    x1 = layer_norm(DN_ALPHA * x_all + mixed, ln1_g[l], ln1_b[l])

    x1_bf = x1.astype(BF16)
    pq = matmul(x1_bf, peer_w_query[l].astype(BF16), 512, 1024, "peer_query")
    gate = peer_routing(pq, peer_sub_keys[l])
    y = peer_experts(x1_bf, peer_u[l].astype(BF16), peer_v[l].astype(BF16), gate, 512, 256)
    x2 = layer_norm(DN_ALPHA * x1 + y, ln2_g[l], ln2_b[l])

    outs_p = [a[None] for a in new_p]
    outs_s = [a[None] for a in new_s]
    return (x2[:NP].reshape(B, T, D_MODEL), x2[NP:].reshape(DB, L, D_MODEL), *outs_p, *outs_s)
```

```python
import functools
import math

import jax
import jax.numpy as jnp
import numpy as np
from jax import lax
from jax.experimental import pallas as pl
from jax.experimental.pallas import tpu as pltpu

D_MODEL = 4096
PAST_LEN = 2048
GDN_DK = 128
GDN_DV = 128
GDN_HEADS = D_MODEL // 2 // GDN_DV
GDN_CONV = 4
GDN_CHUNK = 64
GDN_QK = GDN_HEADS * GDN_DK
GDN_VW = GDN_HEADS * GDN_DV
NSA_DIM = 128
NSA_HEADS = (D_MODEL - GDN_VW) // NSA_DIM
NSA_GROUP = 4
NSA_KV_HEADS = NSA_HEADS // NSA_GROUP
NSA_QW = NSA_HEADS * NSA_DIM
NSA_KVW = NSA_KV_HEADS * NSA_DIM
CMP_STRIDE = 16
CMP_LEN = 2 * CMP_STRIDE
CMP_HIDDEN = NSA_DIM
SLC_BLOCK = 64
SLC_TOPK = 16
WINDOW = 512
ROT_DIM = NSA_DIM // 4
ROPE_THETA = 500000.0
PEER_HEADS = 8
PEER_KEYS = 128
PEER_EXPERTS = PEER_KEYS * PEER_KEYS
PEER_TOPK = 16
PEER_QDIM = 256
DEPTH = 1
DN_ALPHA = (2.0 * DEPTH) ** 0.25
LN_EPS = 1e-5
NEG = -1e30
FORCE = 1e30
IN_SIZES = (GDN_QK, GDN_QK, GDN_VW, GDN_VW, GDN_HEADS, GDN_HEADS, NSA_QW, NSA_KVW, NSA_KVW, NSA_KVW, NSA_KVW,
            NSA_KVW, NSA_KVW, 3 * NSA_HEADS)

LANES = 128
VMEM_LIMIT = 56 * 1024 * 1024

F32 = jnp.float32
BF16 = jnp.bfloat16
NT_DIMS = (((1,), (1,)), ((), ()))


def _matmul_kernel(x_ref, w_ref, o_ref):
    o_ref[...] = jnp.dot(x_ref[...], w_ref[...], preferred_element_type=F32)


def matmul(x, w, tm, tn, name):
    m, k = x.shape
    n = w.shape[1]
    assert m % tm == 0 and n % tn == 0
    return pl.pallas_call(
        _matmul_kernel,
        grid=(n // tn, m // tm),
        in_specs=[pl.BlockSpec((tm, k), lambda j, i: (i, 0)), pl.BlockSpec((k, tn), lambda j, i: (0, j))],
        out_specs=pl.BlockSpec((tm, tn), lambda j, i: (i, j)),
        out_shape=jax.ShapeDtypeStruct((m, n), F32),
        compiler_params=pltpu.CompilerParams(dimension_semantics=("parallel", "parallel"),
                                             vmem_limit_bytes=VMEM_LIMIT),
        name=name,
    )(x, w)


def _gelu_tanh(x):
    return 0.5 * x * (1.0 + jnp.tanh(math.sqrt(2.0 / math.pi) * (x + 0.044715 * (x * x * x))))


def _peer_expert_kernel(x_ref, u_ref, v_ref, g_ref, o_ref):
    @pl.when(pl.program_id(1) == 0)
    def _():
        o_ref[...] = jnp.zeros_like(o_ref)

    s = lax.dot_general(x_ref[...], u_ref[...], NT_DIMS, preferred_element_type=F32)
    h = (g_ref[...].astype(F32) * _gelu_tanh(s)).astype(BF16)
    o_ref[...] += jnp.dot(h, v_ref[...], preferred_element_type=F32)


def peer_experts(x, u, v, gate, tn, te):
    n, d = x.shape
    e = u.shape[0]
    assert n % tn == 0 and e % te == 0
    return pl.pallas_call(
        _peer_expert_kernel,
        grid=(n // tn, e // te),
        in_specs=[pl.BlockSpec((tn, d), lambda i, j: (i, 0)), pl.BlockSpec((te, d), lambda i, j: (j, 0)),
                  pl.BlockSpec((te, d), lambda i, j: (j, 0)), pl.BlockSpec((tn, te), lambda i, j: (i, j))],
        out_specs=pl.BlockSpec((tn, d), lambda i, j: (i, 0)),
        out_shape=jax.ShapeDtypeStruct((n, d), F32),
        compiler_params=pltpu.CompilerParams(dimension_semantics=("parallel", "arbitrary"),
                                             vmem_limit_bytes=VMEM_LIMIT),
        name="peer_experts",
    )(x, u, v, gate)


def layer_norm(x, g, b):
    mu = jnp.mean(x, axis=-1, keepdims=True)
    var = jnp.mean(jnp.square(x - mu), axis=-1, keepdims=True)
    return (x - mu) * lax.rsqrt(var + LN_EPS) * g + b


def rms_norm(x, g):
    return x * lax.rsqrt(jnp.mean(x * x, axis=-1, keepdims=True) + 1e-6) * g


def l2_normalize(x):
    return x * lax.rsqrt(jnp.sum(x * x, axis=-1, keepdims=True) + 1e-6)


def partial_rope(x, pos):
    half = ROT_DIM // 2
    inv = ROPE_THETA ** (-jnp.arange(half, dtype=F32) / half)
    ang = pos.astype(F32)[:, None] * inv[None, :]
    cos = jnp.cos(ang)[None, :, None, :]
    sin = jnp.sin(ang)[None, :, None, :]
    xr = x[..., :ROT_DIM]
    x1, x2 = xr[..., :half], xr[..., half:]
    rot = jnp.concatenate([x1 * cos - x2 * sin, x2 * cos + x1 * sin], axis=-1)
    return jnp.concatenate([rot, x[..., ROT_DIM:]], axis=-1)


def causal_short_conv(x, buf, w):
    T = x.shape[1]
    xp = jnp.concatenate([buf, x], axis=1)
    y = w[0] * xp[:, 0:T]
    for j in range(1, GDN_CONV):
        y = y + w[j] * xp[:, j:j + T]
    return jax.nn.silu(y), xp[:, xp.shape[1] - (GDN_CONV - 1):]


def gated_delta_rule(q, k, v, g, beta, s0):
    B, T, H, dk = q.shape
    dv = v.shape[-1]
    C = T if T <= GDN_CHUNK else math.gcd(T, GDN_CHUNK)
    n = T // C
    q = l2_normalize(q) * (dk ** -0.5)
    k = l2_normalize(k)

    def chunks(a):
        a = a.reshape((B, n, C, H) + a.shape[3:])
        return jnp.moveaxis(a, (1, 3), (0, 2))

    qc, kc, vc, bc = chunks(q), chunks(k), chunks(v), chunks(beta)
    gc = jnp.cumsum(chunks(g), axis=-1)
    incl = jnp.tril(jnp.ones((C, C), bool))
    strict = jnp.tril(jnp.ones((C, C), bool), -1)
    decay = jnp.exp(jnp.where(incl, gc[..., :, None] - gc[..., None, :], -jnp.inf))
    kb = kc * bc[..., None]
    vb = vc * bc[..., None]
    a = jnp.where(strict, jnp.einsum('nbhid,nbhjd->nbhij', kb, kc) * decay, 0.0)
    eye = jnp.eye(C, dtype=a.dtype)
    t_inv = lax.linalg.triangular_solve(eye + a, jnp.broadcast_to(eye, a.shape), left_side=True, lower=True)
    u = t_inv @ vb
    w = t_inv @ (kb * jnp.exp(gc)[..., None])
    qk = jnp.where(incl, jnp.einsum('nbhid,nbhjd->nbhij', qc, kc) * decay, 0.0)
    q_dec = qc * jnp.exp(gc)[..., None]
    k_dec = kc * jnp.exp(gc[..., -1:] - gc)[..., None]
    g_last = jnp.exp(gc[..., -1])

    def step(s, xs):
        qk_i, u_i, w_i, q_i, k_i, gl = xs
        v_new = u_i - w_i @ s
        o = q_i @ s + qk_i @ v_new
        s = s * gl[..., None, None] + jnp.swapaxes(k_i, -1, -2) @ v_new
        return s, o

    s_final, o = lax.scan(step, s0, (qk, u, w, q_dec, k_dec, g_last))
    o = jnp.moveaxis(o, (0, 2), (1, 3)).reshape(B, T, H, dv)
    return o, s_final


def gdn_mixer(q, k, v, z, a, b, conv_buf, s0, conv_w, a_log, dt_bias, norm_w):
    B, T, _ = q.shape
    qkv, conv_new = causal_short_conv(jnp.concatenate([q, k, v], axis=-1), conv_buf, conv_w)
    q, k, v = jnp.split(qkv, [GDN_QK, 2 * GDN_QK], axis=-1)
    g = -jnp.exp(a_log) * jax.nn.softplus(a + dt_bias)
    beta = jax.nn.sigmoid(b)
    o, s_new = gated_delta_rule(q.reshape(B, T, GDN_HEADS, GDN_DK), k.reshape(B, T, GDN_HEADS, GDN_DK),
                                v.reshape(B, T, GDN_HEADS, GDN_DV), g, beta, s0)
    o = rms_norm(o, norm_w) * jax.nn.silu(z.reshape(B, T, GDN_HEADS, GDN_DV))
    return o.reshape(B, T, GDN_VW), conv_new, s_new


GDN_BLOCK = 128
GDN_HG = 2
SUBLANES = 8


def _sigmoid(x):
    return 1.0 / (1.0 + jnp.exp(-x))


SPLIT_PASSES = 3


def _dot_split(a, b):
    a_hi, b_hi = a.astype(BF16), b.astype(BF16)
    out = jnp.dot(a_hi, b_hi, preferred_element_type=F32)
    if SPLIT_PASSES == 1:
        return out
    a_lo = (a - a_hi.astype(F32)).astype(BF16)
    b_lo = (b - b_hi.astype(F32)).astype(BF16)
    return out + (jnp.dot(a_hi, b_lo, preferred_element_type=F32) + jnp.dot(a_lo, b_hi, preferred_element_type=F32))


def _gdn_prompt_kernel(q_ref, k_ref, v_ref, z_ref, cols_ref, rows_ref, wq_ref, wk_ref, wv_ref, nw_ref,
                       o_ref, sfin_ref, s_scr):
    C = GDN_BLOCK
    T = q_ref.shape[0]
    grp = pl.program_id(1)
    s_scr[...] = jnp.zeros_like(s_scr)
    ri = lax.broadcasted_iota(jnp.int32, (C, C), 0)
    ci = lax.broadcasted_iota(jnp.int32, (C, C), 1)
    incl, strict = ri >= ci, ri > ci
    eye = jnp.where(ri == ci, 1.0, 0.0)
    lane = lax.broadcasted_iota(jnp.int32, (C, LANES), 1)

    def conv_silu(ref, w_ref, i, c, r0):
        sl = slice(i * GDN_DK, (i + 1) * GDN_DK)
        prev = ref[pl.ds(pl.multiple_of(jnp.maximum(r0 - SUBLANES, 0), SUBLANES), SUBLANES), sl]
        xp = jnp.concatenate([jnp.where(c > 0, prev, 0.0), ref[pl.ds(r0, C), sl]], axis=0)
        w = w_ref[:, sl]
        y = w[GDN_CONV - 1:GDN_CONV] * xp[SUBLANES:]
        for j in range(GDN_CONV - 1):
            y = y + w[j:j + 1] * pltpu.roll(xp, GDN_CONV - 1 - j, axis=0)[SUBLANES:]
        return y * _sigmoid(y)

    def chunk(c, carry):
        r0 = pl.multiple_of(c * C, C)
        colblk = cols_ref[0, pl.ds(r0, C), :]
        for i in range(GDN_HG):
            h = grp * GDN_HG + i
            beta = jnp.sum(jnp.where(lane == h, colblk, 0.0), axis=1, keepdims=True)
            gc = jnp.sum(jnp.where(lane == GDN_HEADS + h, colblk, 0.0), axis=1, keepdims=True)
            gc_row = rows_ref[0, i, pl.ds(c, 1), :]
            g_last = gc_row[:, C - 1:C]
            q = conv_silu(q_ref, wq_ref, i, c, r0)
            k = conv_silu(k_ref, wk_ref, i, c, r0)
            v = conv_silu(v_ref, wv_ref, i, c, r0)
            q = q * lax.rsqrt(jnp.sum(q * q, axis=1, keepdims=True) + 1e-6) * (GDN_DK ** -0.5)
            k = k * lax.rsqrt(jnp.sum(k * k, axis=1, keepdims=True) + 1e-6)
            decay = jnp.exp(jnp.where(incl, gc - gc_row, -jnp.inf))
            e_gc = jnp.exp(gc)
            kb = k * beta
            k16 = k.astype(BF16)
            a = jnp.where(strict, lax.dot_general(kb.astype(BF16), k16, NT_DIMS, preferred_element_type=F32) * decay,
                          0.0)
            qk = jnp.where(incl, lax.dot_general(q.astype(BF16), k16, NT_DIMS, preferred_element_type=F32) * decay,
                           0.0)
            xp = -a
            t_inv = eye + xp
            for _ in range(int(math.log2(C)) - 1):
                xp = _dot_split(xp, xp)
                t_inv = t_inv + _dot_split(t_inv, xp)
            t16 = t_inv.astype(BF16)
            u = jnp.dot(t16, (v * beta).astype(BF16), preferred_element_type=F32)
            w = jnp.dot(t16, (kb * e_gc).astype(BF16), preferred_element_type=F32)
            s = s_scr[i]
            s16 = s.astype(BF16)
            v_new = u - jnp.dot(w.astype(BF16), s16, preferred_element_type=F32)
            v16 = v_new.astype(BF16)
            o = (jnp.dot((q * e_gc).astype(BF16), s16, preferred_element_type=F32)
                 + jnp.dot(qk.astype(BF16), v16, preferred_element_type=F32))
            k_dec = (k * jnp.exp(g_last - gc)).astype(BF16)
            s_scr[i] = s * jnp.exp(g_last) + lax.dot_general(k_dec, v16, (((0,), (0,)), ((), ())),
                                                             preferred_element_type=F32)
            z = z_ref[pl.ds(r0, C), i * GDN_DV:(i + 1) * GDN_DV]
            o = o * lax.rsqrt(jnp.mean(o * o, axis=1, keepdims=True) + 1e-6) * nw_ref[...]
            o_ref[pl.ds(r0, C), i * GDN_DV:(i + 1) * GDN_DV] = o * (z * _sigmoid(z))
        return carry

    lax.fori_loop(0, T // C, chunk, 0)
    sfin_ref[0] = s_scr[...]


def gdn_prompt(proj, a, b, conv_w, a_log, dt_bias, norm_w, B, T):
    C, HG, H = GDN_BLOCK, GDN_HG, GDN_HEADS
    assert T % C == 0 and H % HG == 0 and 2 * H <= LANES
    g = -jnp.exp(a_log) * jax.nn.softplus(a + dt_bias)
    beta = jax.nn.sigmoid(b)
    gc = jnp.cumsum(g.reshape(B, T // C, C, H), axis=2)
    cols = jnp.concatenate([beta, gc.reshape(B, T, H), jnp.zeros((B, T, LANES - 2 * H), F32)], axis=-1)
    rows = jnp.moveaxis(gc, 3, 1)
    wblk = HG * GDN_DK
    nq = GDN_QK // wblk
    wide = lambda off: pl.BlockSpec((T, wblk), lambda bi, gi: (bi, off * nq + gi))
    taps = lambda off: pl.BlockSpec((GDN_CONV, wblk), lambda bi, gi: (0, off * nq + gi))
    return pl.pallas_call(
        _gdn_prompt_kernel,
        grid=(B, H // HG),
        in_specs=[wide(0), wide(1), wide(2), wide(3),
                  pl.BlockSpec((1, T, LANES), lambda bi, gi: (bi, 0, 0)),
                  pl.BlockSpec((1, HG, T // C, C), lambda bi, gi: (bi, gi, 0, 0)),
                  taps(0), taps(1), taps(2),
                  pl.BlockSpec((1, GDN_DV), lambda bi, gi: (0, 0))],
        out_specs=[pl.BlockSpec((T, wblk), lambda bi, gi: (bi, gi)),
                   pl.BlockSpec((1, HG, GDN_DK, GDN_DV), lambda bi, gi: (bi, gi, 0, 0))],
        out_shape=[jax.ShapeDtypeStruct((B * T, GDN_VW), F32), jax.ShapeDtypeStruct((B, H, GDN_DK, GDN_DV), F32)],
        scratch_shapes=[pltpu.VMEM((HG, GDN_DK, GDN_DV), F32)],
        compiler_params=pltpu.CompilerParams(dimension_semantics=("parallel", "parallel"),
                                             vmem_limit_bytes=VMEM_LIMIT),
        name="gdn_prompt",
    )(proj, proj, proj, proj, cols, rows, conv_w, conv_w, conv_w, norm_w.reshape(1, GDN_DV))


def compress_blocks(x, pos_emb, w1, w2):
    B, Tk, G, d = x.shape
    nsub = Tk // CMP_STRIDE
    sub = x[:, :nsub * CMP_STRIDE].reshape(B, nsub, CMP_STRIDE, G, d)
    blocks = jnp.concatenate([sub[:, :-1], sub[:, 1:]], axis=2) + pos_emb[:, None, :]
    h = jax.nn.gelu(jnp.einsum('bnjgd,jde->bnge', blocks, w1.reshape(CMP_LEN, d, CMP_HIDDEN)))
    return jnp.einsum('bnge,ef->bngf', h, w2)


def selection_matrix(n_cmp, n_slc):
    r = SLC_BLOCK // CMP_STRIDE
    s = CMP_LEN // CMP_STRIDE
    wts = np.convolve(np.ones(r), np.ones(s)).astype(np.float32)
    off = np.arange(n_cmp)[:, None] - r * np.arange(n_slc)[None, :]
    m = np.where((off >= 0) & (off < wts.size), wts[np.clip(off, 0, wts.size - 1)], 0.0)
    return jnp.asarray(m, F32)


def masked_attention(q, k, v, mask):
    d = q.shape[-1]
    s = jnp.einsum('btgrd,bsgd->bgrts', q, k) * (d ** -0.5)
    p = jax.nn.softmax(jnp.where(mask, s, NEG), axis=-1)
    return jnp.einsum('bgrts,bsgd->btgrd', p, v)


def nsa_compressed_selected(q, q_rot, kc_all, vc_all, ks_all, vs_all, q_pos, pos_k, w1_k, w2_k, pos_v, w1_v, w2_v):
    B, T, H, d = q.shape
    Tk = ks_all.shape[1]
    G = NSA_KV_HEADS
    R = H // G
    k_cmp = compress_blocks(kc_all, pos_k, w1_k, w2_k)
    v_cmp = compress_blocks(vc_all, pos_v, w1_v, w2_v)
    n_cmp = k_cmp.shape[1]
    s = jnp.einsum('btgrd,bngd->bgrtn', q.reshape(B, T, G, R, d), k_cmp) * (d ** -0.5)
    blk_end = jnp.arange(n_cmp, dtype=jnp.int32) * CMP_STRIDE + (CMP_LEN - 1)
    cmask = blk_end[None, :] <= q_pos[:, None]
    p = jnp.where(cmask, jax.nn.softmax(jnp.where(cmask, s, NEG), axis=-1), 0.0)
    o_cmp = jnp.einsum('bgrtn,bngd->btgrd', p, v_cmp).reshape(B, T, H, d)
    n_slc = -(-Tk // SLC_BLOCK)
    imp = jnp.einsum('bgrtn,nj->bgtj', p, selection_matrix(n_cmp, n_slc))
    blk = jnp.arange(n_slc, dtype=jnp.int32)
    cur = q_pos // SLC_BLOCK
    valid = blk[None, :] * SLC_BLOCK <= q_pos[:, None]
    forced = (blk[None, :] == 0) | (blk[None, :] == cur[:, None]) | (blk[None, :] == cur[:, None] - 1)
    score = jnp.where(valid, jnp.where(forced, FORCE, imp), NEG)
    si, sj = score[..., :, None], score[..., None, :]
    before = (sj > si) | ((sj == si) & (blk[None, :] < blk[:, None]))
    rank = jnp.sum(before, axis=-1)
    sel = (rank < min(SLC_TOPK, n_slc)) & valid
    key_pos = jnp.arange(Tk, dtype=jnp.int32)
    key_sel = jnp.repeat(sel, SLC_BLOCK, axis=-1)[..., :Tk] & (key_pos[None, :] <= q_pos[:, None])
    o_slc = masked_attention(q_rot.reshape(B, T, G, R, d), ks_all, vs_all, key_sel[:, :, None])
    return o_cmp, o_slc.reshape(B, T, H, d)


def window_attention(q, k, v, q_pos, k_pos):
    B, T, H, d = q.shape
    G = NSA_KV_HEADS
    mask = (k_pos[None, :] <= q_pos[:, None]) & (k_pos[None, :] > q_pos[:, None] - WINDOW)
    return masked_attention(q.reshape(B, T, G, H // G, d), k, v, mask).reshape(B, T, H, d)


ROUTE_TOKENS = LANES


def _top_rows(s, row_id, k, payload=None):
    slot = lax.broadcasted_iota(jnp.int32, (k, s.shape[1]), 0)
    vals = jnp.zeros((k, s.shape[1]), F32)
    picks = jnp.zeros((k, s.shape[1]), F32)
    for t in range(k):
        m = jnp.max(s, axis=0, keepdims=True)
        i = jnp.min(jnp.where(s == m, row_id, jnp.inf), axis=0, keepdims=True)
        hit = row_id == i
        pick = i if payload is None else jnp.sum(jnp.where(hit, payload, 0.0), axis=0, keepdims=True)
        vals = jnp.where(slot == t, m, vals)
        picks = jnp.where(slot == t, pick, picks)
        s = jnp.where(hit, -jnp.inf, s)
    return vals, picks


def _peer_route_kernel(q_ref, keys_ref, gate_ref, e_scr, g_scr):
    T = ROUTE_TOKENS
    half = PEER_QDIM // 2
    key_id = lax.broadcasted_iota(jnp.int32, (PEER_KEYS, T), 0).astype(F32)
    lo = PEER_TOPK // 2
    cand_row = lax.broadcasted_iota(jnp.int32, (lo * PEER_TOPK + lo, T), 0)
    cand_id = jnp.where(cand_row < lo * PEER_TOPK, cand_row,
                        lo * PEER_TOPK + (cand_row - lo * PEER_TOPK) * PEER_TOPK).astype(F32)
    experts, gates = [], []
    for h in range(PEER_HEADS):
        sub = []
        for p in range(2):
            q = q_ref[:, (2 * h + p) * half:(2 * h + p + 1) * half].astype(BF16)
            s = lax.dot_general(keys_ref[2 * h + p], q, NT_DIMS, preferred_element_type=F32)
            sub.append(_top_rows(s, key_id, PEER_TOPK))
        (s1, i1), (s2, i2) = sub
        cand = jnp.concatenate([s1[k:k + 1] + s2 for k in range(lo)] + [s1[lo:] + s2[0:1]], axis=0)
        cand_e = jnp.concatenate([i1[k:k + 1] * float(PEER_KEYS) + i2 for k in range(lo)]
                                 + [i1[lo:] * float(PEER_KEYS) + i2[0:1]], axis=0)
        top_s, top_e = _top_rows(cand, cand_id, PEER_TOPK, payload=cand_e)
        ex = jnp.exp(top_s - top_s[0:1])
        experts.append(top_e)
        gates.append(ex / jnp.sum(ex, axis=0, keepdims=True))
    e_scr[...] = jnp.concatenate(experts, axis=0).T
    g_scr[...] = jnp.concatenate(gates, axis=0).T

    sub_id = lax.broadcasted_iota(jnp.int32, (PEER_KEYS, PEER_HEADS * PEER_TOPK), 0).astype(F32)

    def scatter_token(n, carry):
        e = e_scr[pl.ds(n, 1), :]
        a = jnp.floor(e * (1.0 / PEER_KEYS))
        b = e - a * float(PEER_KEYS)
        lhs = jnp.where(sub_id == a, g_scr[pl.ds(n, 1), :], 0.0).astype(BF16)
        rhs = jnp.where(sub_id == b, 1.0, 0.0).astype(BF16)
        gate_ref[n] = lax.dot_general(lhs, rhs, NT_DIMS, preferred_element_type=F32).astype(BF16)
        return carry

    lax.fori_loop(0, T, scatter_token, 0, unroll=8)


def peer_routing(q, sub_keys):
    n = q.shape[0]
    assert n % ROUTE_TOKENS == 0
    keys = sub_keys.reshape(PEER_HEADS * 2, PEER_KEYS, PEER_QDIM // 2).astype(BF16)
    j = PEER_HEADS * PEER_TOPK
    dense = pl.pallas_call(
        _peer_route_kernel,
        grid=(n // ROUTE_TOKENS,),
        in_specs=[pl.BlockSpec((ROUTE_TOKENS, q.shape[1]), lambda i: (i, 0)),
                  pl.BlockSpec(keys.shape, lambda i: (0, 0, 0))],
        out_specs=pl.BlockSpec((ROUTE_TOKENS, PEER_KEYS, PEER_KEYS), lambda i: (i, 0, 0)),
        out_shape=jax.ShapeDtypeStruct((n, PEER_KEYS, PEER_KEYS), BF16),
        scratch_shapes=[pltpu.VMEM((ROUTE_TOKENS, j), F32), pltpu.VMEM((ROUTE_TOKENS, j), F32)],
        compiler_params=pltpu.CompilerParams(dimension_semantics=("parallel",), vmem_limit_bytes=VMEM_LIMIT),
        name="peer_routing",
    )(q, keys)
    return dense.reshape(n, PEER_EXPERTS)


def _permuted_in_weight(w_in):
    cuts = np.cumsum(IN_SIZES)[:-1].tolist()
    parts = jnp.split(w_in, cuts, axis=-1)
    wide = [parts[i] for i in (0, 1, 2, 3, 6, 7, 8, 9, 10, 11, 12)]
    narrow = [parts[i] for i in (4, 5, 13)]
    n_narrow = sum(p.shape[1] for p in narrow)
    pad = jnp.zeros((w_in.shape[0], LANES - n_narrow), w_in.dtype)
    return jnp.concatenate(wide, axis=-1).astype(BF16), jnp.concatenate(narrow + [pad], axis=-1).astype(BF16)


def _split_projection(wide, narrow, B, T):
    sizes = [IN_SIZES[i] for i in (0, 1, 2, 3, 6, 7, 8, 9, 10, 11, 12)]
    cuts = np.cumsum(sizes)[:-1].tolist()
    gq, gk, gv, gz, nq, kc, vc, ks, vs, kw, vw = [p.reshape(B, T, -1) for p in jnp.split(wide, cuts, axis=-1)]
    ga = narrow[:, :GDN_HEADS].reshape(B, T, -1)
    gb = narrow[:, GDN_HEADS:2 * GDN_HEADS].reshape(B, T, -1)
    ngate = narrow[:, 2 * GDN_HEADS:2 * GDN_HEADS + 3 * NSA_HEADS].reshape(B, T, -1)
    return gq, gk, gv, gz, ga, gb, nq, kc, vc, ks, vs, kw, vw, ngate


def _mixers(proj, B, T, pos, past, win_buf, gdn, weights):
    (conv_w, a_log, dt_bias, norm_w, pos_k, w1_k, w2_k, pos_v, w1_v, w2_v) = weights
    H, G, hd = NSA_HEADS, NSA_KV_HEADS, NSA_DIM
    (gq, gk, gv, gz, ga, gb, nq, kc, vc, ks, vs, kw, vw, ngate) = proj
    if callable(gdn):
        o_gdn, conv_new, ssm_new = gdn(gq, gk, gv, gz, ga, gb)
    else:
        conv_buf, ssm0 = gdn
        o_gdn, conv_new, ssm_new = gdn_mixer(gq, gk, gv, gz, ga, gb, conv_buf, ssm0, conv_w, a_log, dt_bias, norm_w)
    q = nq.reshape(B, T, H, hd)
    q_rot = partial_rope(q, pos)
    kc = kc.reshape(B, T, G, hd)
    vc = vc.reshape(B, T, G, hd)
    ks = partial_rope(ks.reshape(B, T, G, hd), pos)
    vs = vs.reshape(B, T, G, hd)
    kw = partial_rope(kw.reshape(B, T, G, hd), pos)
    vw = vw.reshape(B, T, G, hd)
    if past is None:
        kc_all, vc_all, ks_all, vs_all = kc, vc, ks, vs
    else:
        kc_all, vc_all, ks_all, vs_all = [jnp.concatenate([p, n], axis=1) for p, n in zip(past, (kc, vc, ks, vs))]
    o_cmp, o_slc = nsa_compressed_selected(q, q_rot, kc_all, vc_all, ks_all, vs_all, pos,
                                           pos_k, w1_k, w2_k, pos_v, w1_v, w2_v)
    if win_buf is None:
        o_win = window_attention(q_rot, kw, vw, pos, pos)
        keep = min(WINDOW, T)
        win_k_new, win_v_new = kw[:, T - keep:], vw[:, T - keep:]
    else:
        k_buf, v_buf = win_buf
        Wb = k_buf.shape[1]
        k_all = jnp.concatenate([k_buf, kw], axis=1)
        v_all = jnp.concatenate([v_buf, vw], axis=1)
        k_pos = jnp.concatenate([PAST_LEN - Wb + jnp.arange(Wb, dtype=jnp.int32), pos])
        o_win = window_attention(q_rot, k_all, v_all, pos, k_pos)
        win_k_new, win_v_new = k_all[:, k_all.shape[1] - Wb:], v_all[:, v_all.shape[1] - Wb:]
    gate = jax.nn.sigmoid(ngate).reshape(B, T, 3, H, 1)
    o_nsa = gate[:, :, 0] * o_cmp + gate[:, :, 1] * o_slc + gate[:, :, 2] * o_win
    mix_in = jnp.concatenate([o_gdn, o_nsa.reshape(B, T, NSA_QW)], axis=-1)
    return mix_in.reshape(B * T, D_MODEL), (kc, vc, ks, vs, win_k_new, win_v_new, conv_new, ssm_new)


def kernel(x_prompt, x_sample, cache_cmp_k, cache_cmp_v, cache_slc_k, cache_slc_v, cache_win_k, cache_win_v,
           state_conv, state_ssm, page_table, w_in, gdn_conv_w, gdn_a_log, gdn_dt_bias, gdn_norm_w,
           cmp_pos_k, cmp_w1_k, cmp_w2_k, cmp_pos_v, cmp_w1_v, cmp_w2_v, w_out, ln1_g, ln1_b,
           peer_w_query, peer_sub_keys, peer_u, peer_v, ln2_g, ln2_b):
    assert w_in.shape[0] == DEPTH == 1
    G, hd = NSA_KV_HEADS, NSA_DIM
    B, T, _ = x_prompt.shape
    DB, L, _ = x_sample.shape
    NP, NS = B * T, DB * L
    pos_p = jnp.arange(T, dtype=jnp.int32)
    pos_s = PAST_LEN + jnp.arange(L, dtype=jnp.int32)
    l = 0
    x_all = jnp.concatenate([x_prompt.reshape(NP, D_MODEL), x_sample.reshape(NS, D_MODEL)], axis=0)

    w_wide, w_narrow = _permuted_in_weight(w_in[l])
    x_bf = x_all.astype(BF16)
    proj_wide = matmul(x_bf, w_wide, 512, 1024, "in_proj_wide")
    proj_narrow = matmul(x_bf, w_narrow, 512, LANES, "in_proj_narrow")

    weights = (gdn_conv_w[l], gdn_a_log[l], gdn_dt_bias[l], gdn_norm_w[l], cmp_pos_k[l], cmp_w1_k[l], cmp_w2_k[l],
               cmp_pos_v[l], cmp_w1_v[l], cmp_w2_v[l])
    def gdn_prompt_group(gq, gk, gv, gz, ga, gb):
        o, ssm = gdn_prompt(proj_wide, ga, gb, gdn_conv_w[l], gdn_a_log[l], gdn_dt_bias[l], gdn_norm_w[l], B, T)
        conv_new = jnp.concatenate([gq, gk, gv], axis=-1)[:, T - (GDN_CONV - 1):]
        return o.reshape(B, T, GDN_VW), conv_new, ssm

    mix_p, new_p = _mixers(_split_projection(proj_wide[:NP], proj_narrow[:NP], B, T), B, T, pos_p, None, None,
                           gdn_prompt_group, weights)

    def paged(pool):
        return pool[l][page_table].reshape(DB, -1, G, hd)

    mix_s, new_s = _mixers(_split_projection(proj_wide[NP:], proj_narrow[NP:], DB, L), DB, L, pos_s,
                           (paged(cache_cmp_k), paged(cache_cmp_v), paged(cache_slc_k), paged(cache_slc_v)),
                           (cache_win_k[l], cache_win_v[l]), (state_conv[l], state_ssm[l]), weights)

    mix_in = jnp.concatenate([mix_p, mix_s], axis=0).astype(BF16)
    mixed = matmul(mix_in, w_out[l].astype(BF16), 512, 1024, "out_proj")
    x1 = layer_norm(DN_ALPHA * x_all + mixed, ln1_g[l], ln1_b[l])

    x1_bf = x1.astype(BF16)
    pq = matmul(x1_bf, peer_w_query[l].astype(BF16), 512, 1024, "peer_query")
    gate = peer_routing(pq, peer_sub_keys[l])
    y = peer_experts(x1_bf, peer_u[l].astype(BF16), peer_v[l].astype(BF16), gate, 512, 512)
    x2 = layer_norm(DN_ALPHA * x1 + y, ln2_g[l], ln2_b[l])

    outs_p = [a[None] for a in new_p]
    outs_s = [a[None] for a in new_s]
    return (x2[:NP].reshape(B, T, D_MODEL), x2[NP:].reshape(DB, L, D_MODEL), *outs_p, *outs_s)
```

```python
import functools
import math

import jax
import jax.numpy as jnp
import numpy as np
from jax import lax
from jax.experimental import pallas as pl
from jax.experimental.pallas import tpu as pltpu

D_MODEL = 4096
PAST_LEN = 2048
GDN_DK = 128
GDN_DV = 128
GDN_HEADS = D_MODEL // 2 // GDN_DV
GDN_CONV = 4
GDN_CHUNK = 64
GDN_QK = GDN_HEADS * GDN_DK
GDN_VW = GDN_HEADS * GDN_DV
NSA_DIM = 128
NSA_HEADS = (D_MODEL - GDN_VW) // NSA_DIM
NSA_GROUP = 4
NSA_KV_HEADS = NSA_HEADS // NSA_GROUP
NSA_QW = NSA_HEADS * NSA_DIM
NSA_KVW = NSA_KV_HEADS * NSA_DIM
CMP_STRIDE = 16
CMP_LEN = 2 * CMP_STRIDE
CMP_HIDDEN = NSA_DIM
SLC_BLOCK = 64
SLC_TOPK = 16
WINDOW = 512
ROT_DIM = NSA_DIM // 4
ROPE_THETA = 500000.0
PEER_HEADS = 8
PEER_KEYS = 128
PEER_EXPERTS = PEER_KEYS * PEER_KEYS
PEER_TOPK = 16
PEER_QDIM = 256
DEPTH = 1
DN_ALPHA = (2.0 * DEPTH) ** 0.25
LN_EPS = 1e-5
NEG = -1e30
FORCE = 1e30
IN_SIZES = (GDN_QK, GDN_QK, GDN_VW, GDN_VW, GDN_HEADS, GDN_HEADS, NSA_QW, NSA_KVW, NSA_KVW, NSA_KVW, NSA_KVW,
            NSA_KVW, NSA_KVW, 3 * NSA_HEADS)

LANES = 128
VMEM_LIMIT = 56 * 1024 * 1024

F32 = jnp.float32
BF16 = jnp.bfloat16
NT_DIMS = (((1,), (1,)), ((), ()))


def _matmul_kernel(x_ref, w_ref, o_ref):
    o_ref[...] = jnp.dot(x_ref[...], w_ref[...], preferred_element_type=F32)


def matmul(x, w, tm, tn, name):
    m, k = x.shape
    n = w.shape[1]
    assert m % tm == 0 and n % tn == 0
    return pl.pallas_call(
        _matmul_kernel,
        grid=(n // tn, m // tm),
        in_specs=[pl.BlockSpec((tm, k), lambda j, i: (i, 0)), pl.BlockSpec((k, tn), lambda j, i: (0, j))],
        out_specs=pl.BlockSpec((tm, tn), lambda j, i: (i, j)),
        out_shape=jax.ShapeDtypeStruct((m, n), F32),
        compiler_params=pltpu.CompilerParams(dimension_semantics=("parallel", "parallel"),
                                             vmem_limit_bytes=VMEM_LIMIT),
        name=name,
    )(x, w)


def _gelu_tanh(x):
    return 0.5 * x * (1.0 + jnp.tanh(math.sqrt(2.0 / math.pi) * (x + 0.044715 * (x * x * x))))


def _peer_expert_kernel(x_ref, u_ref, v_ref, g_ref, o_ref):
    @pl.when(pl.program_id(1) == 0)
    def _():
        o_ref[...] = jnp.zeros_like(o_ref)

    s = lax.dot_general(x_ref[...], u_ref[...], NT_DIMS, preferred_element_type=F32)
    h = (g_ref[...].astype(F32) * _gelu_tanh(s)).astype(BF16)
    o_ref[...] += jnp.dot(h, v_ref[...], preferred_element_type=F32)


def peer_experts(x, u, v, gate, tn, te):
    n, d = x.shape
    e = u.shape[0]
    assert n % tn == 0 and e % te == 0
    return pl.pallas_call(
        _peer_expert_kernel,
        grid=(n // tn, e // te),
        in_specs=[pl.BlockSpec((tn, d), lambda i, j: (i, 0)), pl.BlockSpec((te, d), lambda i, j: (j, 0)),
                  pl.BlockSpec((te, d), lambda i, j: (j, 0)), pl.BlockSpec((tn, te), lambda i, j: (i, j))],
        out_specs=pl.BlockSpec((tn, d), lambda i, j: (i, 0)),
        out_shape=jax.ShapeDtypeStruct((n, d), F32),
        compiler_params=pltpu.CompilerParams(dimension_semantics=("parallel", "arbitrary"),
                                             vmem_limit_bytes=VMEM_LIMIT),
        name="peer_experts",
    )(x, u, v, gate)


def layer_norm(x, g, b):
    mu = jnp.mean(x, axis=-1, keepdims=True)
    var = jnp.mean(jnp.square(x - mu), axis=-1, keepdims=True)
    return (x - mu) * lax.rsqrt(var + LN_EPS) * g + b


def rms_norm(x, g):
    return x * lax.rsqrt(jnp.mean(x * x, axis=-1, keepdims=True) + 1e-6) * g


def l2_normalize(x):
    return x * lax.rsqrt(jnp.sum(x * x, axis=-1, keepdims=True) + 1e-6)


def partial_rope(x, pos):
    half = ROT_DIM // 2
    inv = ROPE_THETA ** (-jnp.arange(half, dtype=F32) / half)
    ang = pos.astype(F32)[:, None] * inv[None, :]
    cos = jnp.cos(ang)[None, :, None, :]
    sin = jnp.sin(ang)[None, :, None, :]
    xr = x[..., :ROT_DIM]
    x1, x2 = xr[..., :half], xr[..., half:]
    rot = jnp.concatenate([x1 * cos - x2 * sin, x2 * cos + x1 * sin], axis=-1)
    return jnp.concatenate([rot, x[..., ROT_DIM:]], axis=-1)


def causal_short_conv(x, buf, w):
    T = x.shape[1]
    xp = jnp.concatenate([buf, x], axis=1)
    y = w[0] * xp[:, 0:T]
    for j in range(1, GDN_CONV):
        y = y + w[j] * xp[:, j:j + T]
    return jax.nn.silu(y), xp[:, xp.shape[1] - (GDN_CONV - 1):]


def gated_delta_rule(q, k, v, g, beta, s0):
    B, T, H, dk = q.shape
    dv = v.shape[-1]
    C = T if T <= GDN_CHUNK else math.gcd(T, GDN_CHUNK)
    n = T // C
    q = l2_normalize(q) * (dk ** -0.5)
    k = l2_normalize(k)

    def chunks(a):
        a = a.reshape((B, n, C, H) + a.shape[3:])
        return jnp.moveaxis(a, (1, 3), (0, 2))

    qc, kc, vc, bc = chunks(q), chunks(k), chunks(v), chunks(beta)
    gc = jnp.cumsum(chunks(g), axis=-1)
    incl = jnp.tril(jnp.ones((C, C), bool))
    strict = jnp.tril(jnp.ones((C, C), bool), -1)
    decay = jnp.exp(jnp.where(incl, gc[..., :, None] - gc[..., None, :], -jnp.inf))
    kb = kc * bc[..., None]
    vb = vc * bc[..., None]
    a = jnp.where(strict, jnp.einsum('nbhid,nbhjd->nbhij', kb, kc) * decay, 0.0)
    eye = jnp.eye(C, dtype=a.dtype)
    t_inv = lax.linalg.triangular_solve(eye + a, jnp.broadcast_to(eye, a.shape), left_side=True, lower=True)
    u = t_inv @ vb
    w = t_inv @ (kb * jnp.exp(gc)[..., None])
    qk = jnp.where(incl, jnp.einsum('nbhid,nbhjd->nbhij', qc, kc) * decay, 0.0)
    q_dec = qc * jnp.exp(gc)[..., None]
    k_dec = kc * jnp.exp(gc[..., -1:] - gc)[..., None]
    g_last = jnp.exp(gc[..., -1])

    def step(s, xs):
        qk_i, u_i, w_i, q_i, k_i, gl = xs
        v_new = u_i - w_i @ s
        o = q_i @ s + qk_i @ v_new
        s = s * gl[..., None, None] + jnp.swapaxes(k_i, -1, -2) @ v_new
        return s, o

    s_final, o = lax.scan(step, s0, (qk, u, w, q_dec, k_dec, g_last))
    o = jnp.moveaxis(o, (0, 2), (1, 3)).reshape(B, T, H, dv)
    return o, s_final


def gdn_mixer(q, k, v, z, a, b, conv_buf, s0, conv_w, a_log, dt_bias, norm_w):
    B, T, _ = q.shape
    qkv, conv_new = causal_short_conv(jnp.concatenate([q, k, v], axis=-1), conv_buf, conv_w)
    q, k, v = jnp.split(qkv, [GDN_QK, 2 * GDN_QK], axis=-1)
    g = -jnp.exp(a_log) * jax.nn.softplus(a + dt_bias)
    beta = jax.nn.sigmoid(b)
    o, s_new = gated_delta_rule(q.reshape(B, T, GDN_HEADS, GDN_DK), k.reshape(B, T, GDN_HEADS, GDN_DK),
                                v.reshape(B, T, GDN_HEADS, GDN_DV), g, beta, s0)
    o = rms_norm(o, norm_w) * jax.nn.silu(z.reshape(B, T, GDN_HEADS, GDN_DV))
    return o.reshape(B, T, GDN_VW), conv_new, s_new


GDN_BLOCK = 128
GDN_HG = 2
SUBLANES = 8


def _sigmoid(x):
    return 1.0 / (1.0 + jnp.exp(-x))


SPLIT_PASSES = 3


def _dot_split(a, b):
    a_hi, b_hi = a.astype(BF16), b.astype(BF16)
    out = jnp.dot(a_hi, b_hi, preferred_element_type=F32)
    if SPLIT_PASSES == 1:
        return out
    a_lo = (a - a_hi.astype(F32)).astype(BF16)
    b_lo = (b - b_hi.astype(F32)).astype(BF16)
    return out + (jnp.dot(a_hi, b_lo, preferred_element_type=F32) + jnp.dot(a_lo, b_hi, preferred_element_type=F32))


def _gdn_prompt_kernel(q_ref, k_ref, v_ref, z_ref, cols_ref, rows_ref, wq_ref, wk_ref, wv_ref, nw_ref,
                       o_ref, sfin_ref, s_scr):
    C = GDN_BLOCK
    T = q_ref.shape[0]
    grp = pl.program_id(1)
    s_scr[...] = jnp.zeros_like(s_scr)
    ri = lax.broadcasted_iota(jnp.int32, (C, C), 0)
    ci = lax.broadcasted_iota(jnp.int32, (C, C), 1)
    incl, strict = ri >= ci, ri > ci
    eye = jnp.where(ri == ci, 1.0, 0.0)
    lane = lax.broadcasted_iota(jnp.int32, (C, LANES), 1)

    def conv_silu(ref, w_ref, i, c, r0):
        sl = slice(i * GDN_DK, (i + 1) * GDN_DK)
        prev = ref[pl.ds(pl.multiple_of(jnp.maximum(r0 - SUBLANES, 0), SUBLANES), SUBLANES), sl]
        xp = jnp.concatenate([jnp.where(c > 0, prev, 0.0), ref[pl.ds(r0, C), sl]], axis=0)
        w = w_ref[:, sl]
        y = w[GDN_CONV - 1:GDN_CONV] * xp[SUBLANES:]
        for j in range(GDN_CONV - 1):
            y = y + w[j:j + 1] * pltpu.roll(xp, GDN_CONV - 1 - j, axis=0)[SUBLANES:]
        return y * _sigmoid(y)

    def chunk(c, carry):
        r0 = pl.multiple_of(c * C, C)
        colblk = cols_ref[0, pl.ds(r0, C), :]
        for i in range(GDN_HG):
            h = grp * GDN_HG + i
            beta = jnp.sum(jnp.where(lane == h, colblk, 0.0), axis=1, keepdims=True)
            gc = jnp.sum(jnp.where(lane == GDN_HEADS + h, colblk, 0.0), axis=1, keepdims=True)
            gc_row = rows_ref[0, i, pl.ds(c, 1), :]
            g_last = gc_row[:, C - 1:C]
            q = conv_silu(q_ref, wq_ref, i, c, r0)
            k = conv_silu(k_ref, wk_ref, i, c, r0)
            v = conv_silu(v_ref, wv_ref, i, c, r0)
            q = q * lax.rsqrt(jnp.sum(q * q, axis=1, keepdims=True) + 1e-6) * (GDN_DK ** -0.5)
            k = k * lax.rsqrt(jnp.sum(k * k, axis=1, keepdims=True) + 1e-6)
            decay = jnp.exp(jnp.where(incl, gc - gc_row, -jnp.inf))
            e_gc = jnp.exp(gc)
            kb = k * beta
            k16 = k.astype(BF16)
            a = jnp.where(strict, lax.dot_general(kb.astype(BF16), k16, NT_DIMS, preferred_element_type=F32) * decay,
                          0.0)
            qk = jnp.where(incl, lax.dot_general(q.astype(BF16), k16, NT_DIMS, preferred_element_type=F32) * decay,
                           0.0)
            xp = -a
            t_inv = eye + xp
            for _ in range(int(math.log2(C)) - 1):
                xp = _dot_split(xp, xp)
                t_inv = t_inv + _dot_split(t_inv, xp)
            t16 = t_inv.astype(BF16)
            u = jnp.dot(t16, (v * beta).astype(BF16), preferred_element_type=F32)
            w = jnp.dot(t16, (kb * e_gc).astype(BF16), preferred_element_type=F32)
            s = s_scr[i]
            s16 = s.astype(BF16)
            v_new = u - jnp.dot(w.astype(BF16), s16, preferred_element_type=F32)
            v16 = v_new.astype(BF16)
            o = (jnp.dot((q * e_gc).astype(BF16), s16, preferred_element_type=F32)
                 + jnp.dot(qk.astype(BF16), v16, preferred_element_type=F32))
            k_dec = (k * jnp.exp(g_last - gc)).astype(BF16)
            s_scr[i] = s * jnp.exp(g_last) + lax.dot_general(k_dec, v16, (((0,), (0,)), ((), ())),
                                                             preferred_element_type=F32)
            z = z_ref[pl.ds(r0, C), i * GDN_DV:(i + 1) * GDN_DV]
            o = o * lax.rsqrt(jnp.mean(o * o, axis=1, keepdims=True) + 1e-6) * nw_ref[...]
            o_ref[pl.ds(r0, C), i * GDN_DV:(i + 1) * GDN_DV] = o * (z * _sigmoid(z))
        return carry

    lax.fori_loop(0, T // C, chunk, 0)
    sfin_ref[0] = s_scr[...]


def gdn_prompt(proj, a, b, conv_w, a_log, dt_bias, norm_w, B, T):
    C, HG, H = GDN_BLOCK, GDN_HG, GDN_HEADS
    assert T % C == 0 and H % HG == 0 and 2 * H <= LANES
    g = -jnp.exp(a_log) * jax.nn.softplus(a + dt_bias)
    beta = jax.nn.sigmoid(b)
    gc = jnp.cumsum(g.reshape(B, T // C, C, H), axis=2)
    cols = jnp.concatenate([beta, gc.reshape(B, T, H), jnp.zeros((B, T, LANES - 2 * H), F32)], axis=-1)
    rows = jnp.moveaxis(gc, 3, 1)
    wblk = HG * GDN_DK
    nq = GDN_QK // wblk
    wide = lambda off: pl.BlockSpec((T, wblk), lambda bi, gi: (bi, off * nq + gi))
    taps = lambda off: pl.BlockSpec((GDN_CONV, wblk), lambda bi, gi: (0, off * nq + gi))
    return pl.pallas_call(
        _gdn_prompt_kernel,
        grid=(B, H // HG),
        in_specs=[wide(0), wide(1), wide(2), wide(3),
                  pl.BlockSpec((1, T, LANES), lambda bi, gi: (bi, 0, 0)),
                  pl.BlockSpec((1, HG, T // C, C), lambda bi, gi: (bi, gi, 0, 0)),
                  taps(0), taps(1), taps(2),
                  pl.BlockSpec((1, GDN_DV), lambda bi, gi: (0, 0))],
        out_specs=[pl.BlockSpec((T, wblk), lambda bi, gi: (bi, gi)),
                   pl.BlockSpec((1, HG, GDN_DK, GDN_DV), lambda bi, gi: (bi, gi, 0, 0))],
        out_shape=[jax.ShapeDtypeStruct((B * T, GDN_VW), F32), jax.ShapeDtypeStruct((B, H, GDN_DK, GDN_DV), F32)],
        scratch_shapes=[pltpu.VMEM((HG, GDN_DK, GDN_DV), F32)],
        compiler_params=pltpu.CompilerParams(dimension_semantics=("parallel", "parallel"),
                                             vmem_limit_bytes=VMEM_LIMIT),
        name="gdn_prompt",
    )(proj, proj, proj, proj, cols, rows, conv_w, conv_w, conv_w, norm_w.reshape(1, GDN_DV))


def compress_blocks(x, pos_emb, w1, w2):
    B, Tk, G, d = x.shape
    nsub = Tk // CMP_STRIDE
    sub = x[:, :nsub * CMP_STRIDE].reshape(B, nsub, CMP_STRIDE, G, d)
    blocks = jnp.concatenate([sub[:, :-1], sub[:, 1:]], axis=2) + pos_emb[:, None, :]
    h = jax.nn.gelu(jnp.einsum('bnjgd,jde->bnge', blocks, w1.reshape(CMP_LEN, d, CMP_HIDDEN)))
    return jnp.einsum('bnge,ef->bngf', h, w2)


def selection_weights(n_cmp, n_slc):
    r = SLC_BLOCK // CMP_STRIDE
    s = CMP_LEN // CMP_STRIDE
    wts = np.convolve(np.ones(r), np.ones(s)).astype(np.float32)
    off = np.arange(n_cmp)[:, None] - r * np.arange(n_slc)[None, :]
    return np.where((off >= 0) & (off < wts.size), wts[np.clip(off, 0, wts.size - 1)], 0.0).astype(np.float32)


def selection_matrix(n_cmp, n_slc):
    return jnp.asarray(selection_weights(n_cmp, n_slc), F32)


def masked_attention(q, k, v, mask):
    d = q.shape[-1]
    s = jnp.einsum('btgrd,bsgd->bgrts', q, k) * (d ** -0.5)
    p = jax.nn.softmax(jnp.where(mask, s, NEG), axis=-1)
    return jnp.einsum('bgrts,bsgd->btgrd', p, v)


def nsa_compressed_selected(q, q_rot, kc_all, vc_all, ks_all, vs_all, q_pos, pos_k, w1_k, w2_k, pos_v, w1_v, w2_v):
    B, T, H, d = q.shape
    Tk = ks_all.shape[1]
    G = NSA_KV_HEADS
    R = H // G
    k_cmp = compress_blocks(kc_all, pos_k, w1_k, w2_k)
    v_cmp = compress_blocks(vc_all, pos_v, w1_v, w2_v)
    n_cmp = k_cmp.shape[1]
    s = jnp.einsum('btgrd,bngd->bgrtn', q.reshape(B, T, G, R, d), k_cmp) * (d ** -0.5)
    blk_end = jnp.arange(n_cmp, dtype=jnp.int32) * CMP_STRIDE + (CMP_LEN - 1)
    cmask = blk_end[None, :] <= q_pos[:, None]
    p = jnp.where(cmask, jax.nn.softmax(jnp.where(cmask, s, NEG), axis=-1), 0.0)
    o_cmp = jnp.einsum('bgrtn,bngd->btgrd', p, v_cmp).reshape(B, T, H, d)
    n_slc = -(-Tk // SLC_BLOCK)
    imp = jnp.einsum('bgrtn,nj->bgtj', p, selection_matrix(n_cmp, n_slc))
    blk = jnp.arange(n_slc, dtype=jnp.int32)
    cur = q_pos // SLC_BLOCK
    valid = blk[None, :] * SLC_BLOCK <= q_pos[:, None]
    forced = (blk[None, :] == 0) | (blk[None, :] == cur[:, None]) | (blk[None, :] == cur[:, None] - 1)
    score = jnp.where(valid, jnp.where(forced, FORCE, imp), NEG)
    si, sj = score[..., :, None], score[..., None, :]
    before = (sj > si) | ((sj == si) & (blk[None, :] < blk[:, None]))
    rank = jnp.sum(before, axis=-1)
    sel = (rank < min(SLC_TOPK, n_slc)) & valid
    key_pos = jnp.arange(Tk, dtype=jnp.int32)
    key_sel = jnp.repeat(sel, SLC_BLOCK, axis=-1)[..., :Tk] & (key_pos[None, :] <= q_pos[:, None])
    o_slc = masked_attention(q_rot.reshape(B, T, G, R, d), ks_all, vs_all, key_sel[:, :, None])
    return o_cmp, o_slc.reshape(B, T, H, d)


def window_attention(q, k, v, q_pos, k_pos):
    B, T, H, d = q.shape
    G = NSA_KV_HEADS
    mask = (k_pos[None, :] <= q_pos[:, None]) & (k_pos[None, :] > q_pos[:, None] - WINDOW)
    return masked_attention(q.reshape(B, T, G, H // G, d), k, v, mask).reshape(B, T, H, d)


PAGE_SIZE = 128
N_PAGES = PAST_LEN // PAGE_SIZE
PAGE_ROWS = PAGE_SIZE * NSA_KV_HEADS
SUB_ROWS = CMP_STRIDE * NSA_KV_HEADS
SUBS_PER_PAGE = PAGE_SIZE // CMP_STRIDE
N_SUB = PAST_LEN // CMP_STRIDE
DEC_SEQ = 8
Q_ROWS = DEC_SEQ * NSA_HEADS
NEW_ROWS = DEC_SEQ * NSA_KV_HEADS


def _sample_compress_kernel(pt_ref, *refs):
    k_pages, v_pages = refs[:N_PAGES], refs[N_PAGES:2 * N_PAGES]
    pos_k, w1_k, w2_k, pos_v, w1_v, w2_v, ok_ref, ov_ref = refs[2 * N_PAGES:]
    G = NSA_KV_HEADS

    def compress(pages, pos_ref, w1_ref, w2_ref, out_ref):
        first = jnp.zeros((G * N_SUB, CMP_HIDDEN), F32)
        second = jnp.zeros((G * N_SUB, CMP_HIDDEN), F32)
        for j in range(CMP_STRIDE):
            x = jnp.concatenate([pg[0, pl.ds(j * G + g, SUBS_PER_PAGE, stride=SUB_ROWS), :]
                                 for g in range(G) for pg in pages], axis=0)
            lo = (x + pos_ref[j:j + 1, :]).astype(BF16)
            hi = (x + pos_ref[CMP_STRIDE + j:CMP_STRIDE + j + 1, :]).astype(BF16)
            first = first + jnp.dot(lo, w1_ref[j * NSA_DIM:(j + 1) * NSA_DIM, :], preferred_element_type=F32)
            second = second + jnp.dot(hi, w1_ref[(CMP_STRIDE + j) * NSA_DIM:(CMP_STRIDE + j + 1) * NSA_DIM, :],
                                      preferred_element_type=F32)
        h = first + pltpu.roll(second, G * N_SUB - 1, axis=0)
        out_ref[0] = jnp.dot(_gelu_tanh(h).astype(BF16), w2_ref[...], preferred_element_type=F32)

    compress(k_pages, pos_k, w1_k, w2_k, ok_ref)
    compress(v_pages, pos_v, w1_v, w2_v, ov_ref)


def _paged_specs(n):
    return [pl.BlockSpec((1, PAGE_ROWS, NSA_DIM), lambda b, pt, p=p: (pt[b, p], 0, 0)) for p in range(N_PAGES)] * n


def sample_compress(page_table, cmp_k, cmp_v, pos_k, w1_k, w2_k, pos_v, w1_v, w2_v):
    db = page_table.shape[0]
    full = lambda shape: pl.BlockSpec(shape, lambda b, pt: (0,) * len(shape))
    out = pl.BlockSpec((1, NSA_KV_HEADS * N_SUB, NSA_DIM), lambda b, pt: (b, 0, 0))
    return pl.pallas_call(
        _sample_compress_kernel,
        grid_spec=pltpu.PrefetchScalarGridSpec(
            num_scalar_prefetch=1, grid=(db,),
            in_specs=_paged_specs(1) + _paged_specs(1)
            + [full((CMP_LEN, NSA_DIM)), full((CMP_LEN * NSA_DIM, CMP_HIDDEN)), full((CMP_HIDDEN, NSA_DIM))] * 2,
            out_specs=[out, out]),
        out_shape=[jax.ShapeDtypeStruct((db, NSA_KV_HEADS * N_SUB, NSA_DIM), F32)] * 2,
        compiler_params=pltpu.CompilerParams(dimension_semantics=("parallel",), vmem_limit_bytes=VMEM_LIMIT),
        name="sample_compress",
    )(page_table, *([cmp_k] * N_PAGES), *([cmp_v] * N_PAGES), pos_k, w1_k.astype(BF16), w2_k.astype(BF16),
      pos_v, w1_v.astype(BF16), w2_v.astype(BF16))


def _sample_attention_kernel(pt_ref, *refs):
    k_pages, v_pages = refs[:N_PAGES], refs[N_PAGES:2 * N_PAGES]
    (wk_ref, wv_ref, kc_ref, vc_ref, q_ref, qr_ref, ksn_ref, vsn_ref, kwn_ref, vwn_ref, m_ref,
     o_ref, s_scr) = refs[2 * N_PAGES:]
    G, R = NSA_KV_HEADS, NSA_GROUP
    scale = NSA_DIM ** -0.5
    n_slc = -(-(PAST_LEN + DEC_SEQ) // SLC_BLOCK)
    blocks_per_page = PAGE_SIZE // SLC_BLOCK

    def iota(shape, axis):
        return lax.broadcasted_iota(jnp.int32, shape, axis)

    def q_side(width):
        r = iota((Q_ROWS, width), 0)
        return r // NSA_HEADS, (r % NSA_HEADS) // R

    def scores(q16, keys):
        return lax.dot_general(q16, keys.astype(BF16), NT_DIMS, preferred_element_type=F32) * scale

    def pad_new(ref):
        return jnp.concatenate([ref[...], jnp.zeros((LANES - NEW_ROWS, NSA_DIM), F32)], axis=0).astype(BF16)

    q16, qr16 = q_ref[...].astype(BF16), qr_ref[...].astype(BF16)

    tq, gq = q_side(G * N_SUB)
    c = iota((Q_ROWS, G * N_SUB), 1)
    ok = (c // N_SUB == gq) & (c % N_SUB < N_SUB - 1)
    s = jnp.where(ok, scores(q16, kc_ref[0]), NEG)
    e = jnp.exp(s - jnp.max(s, axis=1, keepdims=True))
    p = jnp.where(ok, e / jnp.sum(e, axis=1, keepdims=True), 0.0)
    p16 = p.astype(BF16)
    o_ref[0] = jnp.dot(p16, vc_ref[0].astype(BF16), preferred_element_type=F32)

    imp_head = jnp.dot(p16, m_ref[...], preferred_element_type=F32)
    tq, gq = q_side(Q_ROWS)
    cq = iota((Q_ROWS, Q_ROWS), 1)
    same_group = jnp.where((cq // NSA_HEADS == tq) & ((cq % NSA_HEADS) // R == gq), 1.0, 0.0)
    imp = _dot_split(same_group, imp_head)
    tq, gq = q_side(LANES)
    blk = iota((Q_ROWS, LANES), 1)
    pos = PAST_LEN + tq
    cur = pos // SLC_BLOCK
    valid = (blk * SLC_BLOCK <= pos) & (blk < n_slc)
    forced = (blk == 0) | (blk == cur) | (blk == cur - 1)
    score = jnp.where(valid, jnp.where(forced, FORCE, imp), NEG)
    rank = jnp.zeros((Q_ROWS, LANES), F32)
    for i in range(n_slc):
        si = score[:, i:i + 1]
        rank = rank + jnp.where((si > score) | ((si == score) & (i < blk)), 1.0, 0.0)
    sel = jnp.where((rank < SLC_TOPK) & valid, 1.0, 0.0)

    def new_mask():
        tq, gq = q_side(LANES)
        c = iota((Q_ROWS, LANES), 1)
        return (c < NEW_ROWS) & (c % G == gq) & (c // G <= tq)

    tq, gq = q_side(PAGE_ROWS)
    c = iota((Q_ROWS, PAGE_ROWS), 1)
    own_head = c % G == gq
    first_block = c // G < SLC_BLOCK
    for pg in range(N_PAGES):
        picked = jnp.where(first_block, sel[:, blocks_per_page * pg:blocks_per_page * pg + 1],
                           sel[:, blocks_per_page * pg + 1:blocks_per_page * pg + 2])
        s_scr[:, pg * PAGE_ROWS:(pg + 1) * PAGE_ROWS] = jnp.where(own_head & (picked > 0.5),
                                                                   scores(qr16, k_pages[pg][0]), NEG)
    past_cols = N_PAGES * PAGE_ROWS
    new_block = PAST_LEN // SLC_BLOCK
    s_scr[:, past_cols:past_cols + LANES] = jnp.where(new_mask() & (sel[:, new_block:new_block + 1] > 0.5),
                                                      scores(qr16, pad_new(ksn_ref)), NEG)
    s = s_scr[...]
    e = jnp.exp(s - jnp.max(s, axis=1, keepdims=True))
    acc = jnp.dot(e[:, past_cols:].astype(BF16), pad_new(vsn_ref), preferred_element_type=F32)
    for pg in range(N_PAGES):
        acc = acc + jnp.dot(e[:, pg * PAGE_ROWS:(pg + 1) * PAGE_ROWS].astype(BF16), v_pages[pg][0].astype(BF16),
                            preferred_element_type=F32)
    o_ref[1] = acc / jnp.sum(e, axis=1, keepdims=True)

    wrows = WINDOW * G
    tq, gq = q_side(wrows)
    c = iota((Q_ROWS, wrows), 1)
    s_buf = jnp.where((c % G == gq) & (c // G > tq), scores(qr16, wk_ref[0]), NEG)
    s_new = jnp.where(new_mask(), scores(qr16, pad_new(kwn_ref)), NEG)
    m = jnp.maximum(jnp.max(s_buf, axis=1, keepdims=True), jnp.max(s_new, axis=1, keepdims=True))
    e_buf, e_new = jnp.exp(s_buf - m), jnp.exp(s_new - m)
    acc = (jnp.dot(e_buf.astype(BF16), wv_ref[0].astype(BF16), preferred_element_type=F32)
           + jnp.dot(e_new.astype(BF16), pad_new(vwn_ref), preferred_element_type=F32))
    o_ref[2] = acc / (jnp.sum(e_buf, axis=1, keepdims=True) + jnp.sum(e_new, axis=1, keepdims=True))


def sample_attention(page_table, slc_k, slc_v, win_k, win_v, k_cmp, v_cmp, q, q_rot, ks_new, vs_new, kw_new, vw_new):
    db = page_table.shape[0]
    assert WINDOW == win_k.shape[1] // NSA_KV_HEADS and q.shape[0] == db * Q_ROWS
    n_cmp = N_SUB - 1
    n_slc = -(-(PAST_LEN + DEC_SEQ) // SLC_BLOCK)
    sel_m = np.zeros((NSA_KV_HEADS, N_SUB, LANES), np.float32)
    sel_m[:, :n_cmp, :n_slc] = selection_weights(n_cmp, n_slc)[None]
    sel_m = jnp.asarray(sel_m.reshape(NSA_KV_HEADS * N_SUB, LANES), BF16)
    per_seq = lambda rows: pl.BlockSpec((1, rows, NSA_DIM), lambda b, pt: (b, 0, 0))
    flat = lambda rows: pl.BlockSpec((rows, NSA_DIM), lambda b, pt: (b, 0))
    return pl.pallas_call(
        _sample_attention_kernel,
        grid_spec=pltpu.PrefetchScalarGridSpec(
            num_scalar_prefetch=1, grid=(db,),
            in_specs=_paged_specs(1) + _paged_specs(1)
            + [per_seq(WINDOW * NSA_KV_HEADS)] * 2 + [per_seq(NSA_KV_HEADS * N_SUB)] * 2
            + [flat(Q_ROWS)] * 2 + [flat(NEW_ROWS)] * 4
            + [pl.BlockSpec((NSA_KV_HEADS * N_SUB, LANES), lambda b, pt: (0, 0))],
            out_specs=pl.BlockSpec((3, Q_ROWS, NSA_DIM), lambda b, pt: (0, b, 0)),
            scratch_shapes=[pltpu.VMEM((Q_ROWS, N_PAGES * PAGE_ROWS + LANES), F32)]),
        out_shape=jax.ShapeDtypeStruct((3, db * Q_ROWS, NSA_DIM), F32),
        compiler_params=pltpu.CompilerParams(dimension_semantics=("parallel",), vmem_limit_bytes=VMEM_LIMIT),
        name="sample_attention",
    )(page_table, *([slc_k] * N_PAGES), *([slc_v] * N_PAGES), win_k, win_v, k_cmp, v_cmp, q, q_rot,
      ks_new, vs_new, kw_new, vw_new, sel_m)


ROUTE_TOKENS = LANES


def _top_rows(s, row_id, k, payload=None):
    slot = lax.broadcasted_iota(jnp.int32, (k, s.shape[1]), 0)
    vals = jnp.zeros((k, s.shape[1]), F32)
    picks = jnp.zeros((k, s.shape[1]), F32)
    for t in range(k):
        m = jnp.max(s, axis=0, keepdims=True)
        i = jnp.min(jnp.where(s == m, row_id, jnp.inf), axis=0, keepdims=True)
        hit = row_id == i
        pick = i if payload is None else jnp.sum(jnp.where(hit, payload, 0.0), axis=0, keepdims=True)
        vals = jnp.where(slot == t, m, vals)
        picks = jnp.where(slot == t, pick, picks)
        s = jnp.where(hit, -jnp.inf, s)
    return vals, picks


def _peer_route_kernel(q_ref, keys_ref, gate_ref, e_scr, g_scr):
    T = ROUTE_TOKENS
    half = PEER_QDIM // 2
    key_id = lax.broadcasted_iota(jnp.int32, (PEER_KEYS, T), 0).astype(F32)
    lo = PEER_TOPK // 2
    cand_row = lax.broadcasted_iota(jnp.int32, (lo * PEER_TOPK + lo, T), 0)
    cand_id = jnp.where(cand_row < lo * PEER_TOPK, cand_row,
                        lo * PEER_TOPK + (cand_row - lo * PEER_TOPK) * PEER_TOPK).astype(F32)
    experts, gates = [], []
    for h in range(PEER_HEADS):
        sub = []
        for p in range(2):
            q = q_ref[:, (2 * h + p) * half:(2 * h + p + 1) * half].astype(BF16)
            s = lax.dot_general(keys_ref[2 * h + p], q, NT_DIMS, preferred_element_type=F32)
            sub.append(_top_rows(s, key_id, PEER_TOPK))
        (s1, i1), (s2, i2) = sub
        cand = jnp.concatenate([s1[k:k + 1] + s2 for k in range(lo)] + [s1[lo:] + s2[0:1]], axis=0)
        cand_e = jnp.concatenate([i1[k:k + 1] * float(PEER_KEYS) + i2 for k in range(lo)]
                                 + [i1[lo:] * float(PEER_KEYS) + i2[0:1]], axis=0)
        top_s, top_e = _top_rows(cand, cand_id, PEER_TOPK, payload=cand_e)
        ex = jnp.exp(top_s - top_s[0:1])
        experts.append(top_e)
        gates.append(ex / jnp.sum(ex, axis=0, keepdims=True))
    e_scr[...] = jnp.concatenate(experts, axis=0).T
    g_scr[...] = jnp.concatenate(gates, axis=0).T

    sub_id = lax.broadcasted_iota(jnp.int32, (PEER_KEYS, PEER_HEADS * PEER_TOPK), 0).astype(F32)

    def scatter_token(n, carry):
        e = e_scr[pl.ds(n, 1), :]
        a = jnp.floor(e * (1.0 / PEER_KEYS))
        b = e - a * float(PEER_KEYS)
        lhs = jnp.where(sub_id == a, g_scr[pl.ds(n, 1), :], 0.0).astype(BF16)
        rhs = jnp.where(sub_id == b, 1.0, 0.0).astype(BF16)
        gate_ref[n] = lax.dot_general(lhs, rhs, NT_DIMS, preferred_element_type=F32).astype(BF16)
        return carry

    lax.fori_loop(0, T, scatter_token, 0, unroll=8)


def peer_routing(q, sub_keys):
    n = q.shape[0]
    assert n % ROUTE_TOKENS == 0
    keys = sub_keys.reshape(PEER_HEADS * 2, PEER_KEYS, PEER_QDIM // 2).astype(BF16)
    j = PEER_HEADS * PEER_TOPK
    dense = pl.pallas_call(
        _peer_route_kernel,
        grid=(n // ROUTE_TOKENS,),
        in_specs=[pl.BlockSpec((ROUTE_TOKENS, q.shape[1]), lambda i: (i, 0)),
                  pl.BlockSpec(keys.shape, lambda i: (0, 0, 0))],
        out_specs=pl.BlockSpec((ROUTE_TOKENS, PEER_KEYS, PEER_KEYS), lambda i: (i, 0, 0)),
        out_shape=jax.ShapeDtypeStruct((n, PEER_KEYS, PEER_KEYS), BF16),
        scratch_shapes=[pltpu.VMEM((ROUTE_TOKENS, j), F32), pltpu.VMEM((ROUTE_TOKENS, j), F32)],
        compiler_params=pltpu.CompilerParams(dimension_semantics=("parallel",), vmem_limit_bytes=VMEM_LIMIT),
        name="peer_routing",
    )(q, keys)
    return dense.reshape(n, PEER_EXPERTS)


def _permuted_in_weight(w_in):
    cuts = np.cumsum(IN_SIZES)[:-1].tolist()
    parts = jnp.split(w_in, cuts, axis=-1)
    wide = [parts[i] for i in (0, 1, 2, 3, 6, 7, 8, 9, 10, 11, 12)]
    narrow = [parts[i] for i in (4, 5, 13)]
    n_narrow = sum(p.shape[1] for p in narrow)
    pad = jnp.zeros((w_in.shape[0], LANES - n_narrow), w_in.dtype)
    return jnp.concatenate(wide, axis=-1).astype(BF16), jnp.concatenate(narrow + [pad], axis=-1).astype(BF16)


def _split_projection(wide, narrow, B, T):
    sizes = [IN_SIZES[i] for i in (0, 1, 2, 3, 6, 7, 8, 9, 10, 11, 12)]
    cuts = np.cumsum(sizes)[:-1].tolist()
    gq, gk, gv, gz, nq, kc, vc, ks, vs, kw, vw = [p.reshape(B, T, -1) for p in jnp.split(wide, cuts, axis=-1)]
    ga = narrow[:, :GDN_HEADS].reshape(B, T, -1)
    gb = narrow[:, GDN_HEADS:2 * GDN_HEADS].reshape(B, T, -1)
    ngate = narrow[:, 2 * GDN_HEADS:2 * GDN_HEADS + 3 * NSA_HEADS].reshape(B, T, -1)
    return gq, gk, gv, gz, ga, gb, nq, kc, vc, ks, vs, kw, vw, ngate


def _mixers(proj, B, T, pos, past, win_buf, gdn, weights):
    (conv_w, a_log, dt_bias, norm_w, pos_k, w1_k, w2_k, pos_v, w1_v, w2_v) = weights
    H, G, hd = NSA_HEADS, NSA_KV_HEADS, NSA_DIM
    (gq, gk, gv, gz, ga, gb, nq, kc, vc, ks, vs, kw, vw, ngate) = proj
    if callable(gdn):
        o_gdn, conv_new, ssm_new = gdn(gq, gk, gv, gz, ga, gb)
    else:
        conv_buf, ssm0 = gdn
        o_gdn, conv_new, ssm_new = gdn_mixer(gq, gk, gv, gz, ga, gb, conv_buf, ssm0, conv_w, a_log, dt_bias, norm_w)
    q = nq.reshape(B, T, H, hd)
    q_rot = partial_rope(q, pos)
    kc = kc.reshape(B, T, G, hd)
    vc = vc.reshape(B, T, G, hd)
    ks = partial_rope(ks.reshape(B, T, G, hd), pos)
    vs = vs.reshape(B, T, G, hd)
    kw = partial_rope(kw.reshape(B, T, G, hd), pos)
    vw = vw.reshape(B, T, G, hd)
    if past is None:
        kc_all, vc_all, ks_all, vs_all = kc, vc, ks, vs
    else:
        kc_all, vc_all, ks_all, vs_all = [jnp.concatenate([p, n], axis=1) for p, n in zip(past, (kc, vc, ks, vs))]
    o_cmp, o_slc = nsa_compressed_selected(q, q_rot, kc_all, vc_all, ks_all, vs_all, pos,
                                           pos_k, w1_k, w2_k, pos_v, w1_v, w2_v)
    if win_buf is None:
        o_win = window_attention(q_rot, kw, vw, pos, pos)
        keep = min(WINDOW, T)
        win_k_new, win_v_new = kw[:, T - keep:], vw[:, T - keep:]
    else:
        k_buf, v_buf = win_buf
        Wb = k_buf.shape[1]
        k_all = jnp.concatenate([k_buf, kw], axis=1)
        v_all = jnp.concatenate([v_buf, vw], axis=1)
        k_pos = jnp.concatenate([PAST_LEN - Wb + jnp.arange(Wb, dtype=jnp.int32), pos])
        o_win = window_attention(q_rot, k_all, v_all, pos, k_pos)
        win_k_new, win_v_new = k_all[:, k_all.shape[1] - Wb:], v_all[:, v_all.shape[1] - Wb:]
    gate = jax.nn.sigmoid(ngate).reshape(B, T, 3, H, 1)
    o_nsa = gate[:, :, 0] * o_cmp + gate[:, :, 1] * o_slc + gate[:, :, 2] * o_win
    mix_in = jnp.concatenate([o_gdn, o_nsa.reshape(B, T, NSA_QW)], axis=-1)
    return mix_in.reshape(B * T, D_MODEL), (kc, vc, ks, vs, win_k_new, win_v_new, conv_new, ssm_new)


def _sample_group(proj, DB, L, pos, page_table, pools, win_buf, gdn_state, weights):
    (conv_w, a_log, dt_bias, norm_w, pos_k, w1_k, w2_k, pos_v, w1_v, w2_v) = weights
    H, G, hd = NSA_HEADS, NSA_KV_HEADS, NSA_DIM
    assert L == DEC_SEQ and page_table.shape[1] == N_PAGES and pools[0].shape[1] == PAGE_SIZE
    (gq, gk, gv, gz, ga, gb, nq, kc, vc, ks, vs, kw, vw, ngate) = proj
    conv_buf, ssm0 = gdn_state
    o_gdn, conv_new, ssm_new = gdn_mixer(gq, gk, gv, gz, ga, gb, conv_buf, ssm0, conv_w, a_log, dt_bias, norm_w)
    q = nq.reshape(DB, L, H, hd)
    q_rot = partial_rope(q, pos)
    kc = kc.reshape(DB, L, G, hd)
    vc = vc.reshape(DB, L, G, hd)
    ks = partial_rope(ks.reshape(DB, L, G, hd), pos)
    vs = vs.reshape(DB, L, G, hd)
    kw = partial_rope(kw.reshape(DB, L, G, hd), pos)
    vw = vw.reshape(DB, L, G, hd)
    rows = lambda pool: pool.reshape(pool.shape[0], PAGE_ROWS, hd)
    cmp_k, cmp_v, slc_k, slc_v = [rows(p) for p in pools]
    k_buf, v_buf = win_buf
    k_cmp, v_cmp = sample_compress(page_table, cmp_k, cmp_v, pos_k, w1_k, w2_k, pos_v, w1_v, w2_v)
    new_rows = lambda a: a.reshape(DB * L * G, hd)
    o3 = sample_attention(page_table, slc_k, slc_v, k_buf.reshape(DB, WINDOW * G, hd), v_buf.reshape(DB, WINDOW * G, hd),
                          k_cmp, v_cmp, q.reshape(DB * L * H, hd), q_rot.reshape(DB * L * H, hd),
                          new_rows(ks), new_rows(vs), new_rows(kw), new_rows(vw)).reshape(3, DB, L, H, hd)
    gate = jax.nn.sigmoid(ngate).reshape(DB, L, 3, H, 1)
    o_nsa = gate[:, :, 0] * o3[0] + gate[:, :, 1] * o3[1] + gate[:, :, 2] * o3[2]
    win_k_new = jnp.concatenate([k_buf, kw], axis=1)[:, L:]
    win_v_new = jnp.concatenate([v_buf, vw], axis=1)[:, L:]
    mix_in = jnp.concatenate([o_gdn, o_nsa.reshape(DB, L, NSA_QW)], axis=-1)
    return mix_in.reshape(DB * L, D_MODEL), (kc, vc, ks, vs, win_k_new, win_v_new, conv_new, ssm_new)


def kernel(x_prompt, x_sample, cache_cmp_k, cache_cmp_v, cache_slc_k, cache_slc_v, cache_win_k, cache_win_v,
           state_conv, state_ssm, page_table, w_in, gdn_conv_w, gdn_a_log, gdn_dt_bias, gdn_norm_w,
           cmp_pos_k, cmp_w1_k, cmp_w2_k, cmp_pos_v, cmp_w1_v, cmp_w2_v, w_out, ln1_g, ln1_b,
           peer_w_query, peer_sub_keys, peer_u, peer_v, ln2_g, ln2_b):
    assert w_in.shape[0] == DEPTH == 1
    G, hd = NSA_KV_HEADS, NSA_DIM
    B, T, _ = x_prompt.shape
    DB, L, _ = x_sample.shape
    NP, NS = B * T, DB * L
    pos_p = jnp.arange(T, dtype=jnp.int32)
    pos_s = PAST_LEN + jnp.arange(L, dtype=jnp.int32)
    l = 0
    x_all = jnp.concatenate([x_prompt.reshape(NP, D_MODEL), x_sample.reshape(NS, D_MODEL)], axis=0)

    w_wide, w_narrow = _permuted_in_weight(w_in[l])
    x_bf = x_all.astype(BF16)
    proj_wide = matmul(x_bf, w_wide, 512, 1024, "in_proj_wide")
    proj_narrow = matmul(x_bf, w_narrow, 512, LANES, "in_proj_narrow")

    weights = (gdn_conv_w[l], gdn_a_log[l], gdn_dt_bias[l], gdn_norm_w[l], cmp_pos_k[l], cmp_w1_k[l], cmp_w2_k[l],
               cmp_pos_v[l], cmp_w1_v[l], cmp_w2_v[l])
    def gdn_prompt_group(gq, gk, gv, gz, ga, gb):
        o, ssm = gdn_prompt(proj_wide, ga, gb, gdn_conv_w[l], gdn_a_log[l], gdn_dt_bias[l], gdn_norm_w[l], B, T)
        conv_new = jnp.concatenate([gq, gk, gv], axis=-1)[:, T - (GDN_CONV - 1):]
        return o.reshape(B, T, GDN_VW), conv_new, ssm

    mix_p, new_p = _mixers(_split_projection(proj_wide[:NP], proj_narrow[:NP], B, T), B, T, pos_p, None, None,
                           gdn_prompt_group, weights)

    mix_s, new_s = _sample_group(_split_projection(proj_wide[NP:], proj_narrow[NP:], DB, L), DB, L, pos_s, page_table,
                                 (cache_cmp_k[l], cache_cmp_v[l], cache_slc_k[l], cache_slc_v[l]),
                                 (cache_win_k[l], cache_win_v[l]), (state_conv[l], state_ssm[l]), weights)

    mix_in = jnp.concatenate([mix_p, mix_s], axis=0).astype(BF16)
    mixed = matmul(mix_in, w_out[l].astype(BF16), 512, 1024, "out_proj")
    x1 = layer_norm(DN_ALPHA * x_all + mixed, ln1_g[l], ln1_b[l])

    x1_bf = x1.astype(BF16)
    pq = matmul(x1_bf, peer_w_query[l].astype(BF16), 512, 1024, "peer_query")
    gate = peer_routing(pq, peer_sub_keys[l])
    y = peer_experts(x1_bf, peer_u[l].astype(BF16), peer_v[l].astype(BF16), gate, 512, 512)
    x2 = layer_norm(DN_ALPHA * x1 + y, ln2_g[l], ln2_b[l])

    outs_p = [a[None] for a in new_p]
    outs_s = [a[None] for a in new_s]
    return (x2[:NP].reshape(B, T, D_MODEL), x2[NP:].reshape(DB, L, D_MODEL), *outs_p, *outs_s)
```

```python
import functools
import math

import jax
import jax.numpy as jnp
import numpy as np
from jax import lax
from jax.experimental import pallas as pl
from jax.experimental.pallas import tpu as pltpu

D_MODEL = 4096
PAST_LEN = 2048
GDN_DK = 128
GDN_DV = 128
GDN_HEADS = D_MODEL // 2 // GDN_DV
GDN_CONV = 4
GDN_CHUNK = 64
GDN_QK = GDN_HEADS * GDN_DK
GDN_VW = GDN_HEADS * GDN_DV
NSA_DIM = 128
NSA_HEADS = (D_MODEL - GDN_VW) // NSA_DIM
NSA_GROUP = 4
NSA_KV_HEADS = NSA_HEADS // NSA_GROUP
NSA_QW = NSA_HEADS * NSA_DIM
NSA_KVW = NSA_KV_HEADS * NSA_DIM
CMP_STRIDE = 16
CMP_LEN = 2 * CMP_STRIDE
CMP_HIDDEN = NSA_DIM
SLC_BLOCK = 64
SLC_TOPK = 16
WINDOW = 512
ROT_DIM = NSA_DIM // 4
ROPE_THETA = 500000.0
PEER_HEADS = 8
PEER_KEYS = 128
PEER_EXPERTS = PEER_KEYS * PEER_KEYS
PEER_TOPK = 16
PEER_QDIM = 256
DEPTH = 1
DN_ALPHA = (2.0 * DEPTH) ** 0.25
LN_EPS = 1e-5
NEG = -1e30
FORCE = 1e30
IN_SIZES = (GDN_QK, GDN_QK, GDN_VW, GDN_VW, GDN_HEADS, GDN_HEADS, NSA_QW, NSA_KVW, NSA_KVW, NSA_KVW, NSA_KVW,
            NSA_KVW, NSA_KVW, 3 * NSA_HEADS)

LANES = 128
VMEM_LIMIT = 56 * 1024 * 1024

F32 = jnp.float32
BF16 = jnp.bfloat16
NT_DIMS = (((1,), (1,)), ((), ()))


def _matmul_kernel(x_ref, w_ref, o_ref):
    o_ref[...] = jnp.dot(x_ref[...], w_ref[...], preferred_element_type=F32)


def matmul(x, w, tm, tn, name):
    m, k = x.shape
    n = w.shape[1]
    assert m % tm == 0 and n % tn == 0
    return pl.pallas_call(
        _matmul_kernel,
        grid=(n // tn, m // tm),
        in_specs=[pl.BlockSpec((tm, k), lambda j, i: (i, 0)), pl.BlockSpec((k, tn), lambda j, i: (0, j))],
        out_specs=pl.BlockSpec((tm, tn), lambda j, i: (i, j)),
        out_shape=jax.ShapeDtypeStruct((m, n), F32),
        compiler_params=pltpu.CompilerParams(dimension_semantics=("parallel", "parallel"),
                                             vmem_limit_bytes=VMEM_LIMIT),
        name=name,
    )(x, w)


def _gelu_tanh(x):
    return 0.5 * x * (1.0 + jnp.tanh(math.sqrt(2.0 / math.pi) * (x + 0.044715 * (x * x * x))))


def _peer_expert_kernel(x_ref, u_ref, v_ref, g_ref, o_ref):
    @pl.when(pl.program_id(1) == 0)
    def _():
        o_ref[...] = jnp.zeros_like(o_ref)

    s = lax.dot_general(x_ref[...], u_ref[...], NT_DIMS, preferred_element_type=F32)
    g = jnp.concatenate([g_ref[:, a, :] for a in range(g_ref.shape[1])], axis=1)
    h = (g * _gelu_tanh(s)).astype(BF16)
    o_ref[...] += jnp.dot(h, v_ref[...], preferred_element_type=F32)


def peer_experts(x, u, v, gate, tn, te):
    n, d = x.shape
    e = u.shape[0]
    assert n % tn == 0 and e % te == 0 and te % PEER_KEYS == 0
    once = pl.Buffered(1)
    return pl.pallas_call(
        _peer_expert_kernel,
        grid=(n // tn, e // te),
        in_specs=[pl.BlockSpec((tn, d), lambda i, j: (i, 0), pipeline_mode=once),
                  pl.BlockSpec((te, d), lambda i, j: (j, 0)), pl.BlockSpec((te, d), lambda i, j: (j, 0)),
                  pl.BlockSpec((tn, te // PEER_KEYS, PEER_KEYS), lambda i, j: (i, j, 0))],
        out_specs=pl.BlockSpec((tn, d), lambda i, j: (i, 0), pipeline_mode=once),
        out_shape=jax.ShapeDtypeStruct((n, d), F32),
        compiler_params=pltpu.CompilerParams(dimension_semantics=("parallel", "arbitrary"),
                                             vmem_limit_bytes=VMEM_LIMIT),
        name="peer_experts",
    )(x, u, v, gate)


def layer_norm(x, g, b):
    mu = jnp.mean(x, axis=-1, keepdims=True)
    var = jnp.mean(jnp.square(x - mu), axis=-1, keepdims=True)
    return (x - mu) * lax.rsqrt(var + LN_EPS) * g + b


def rms_norm(x, g):
    return x * lax.rsqrt(jnp.mean(x * x, axis=-1, keepdims=True) + 1e-6) * g


def l2_normalize(x):
    return x * lax.rsqrt(jnp.sum(x * x, axis=-1, keepdims=True) + 1e-6)


def partial_rope(x, pos):
    half = ROT_DIM // 2
    inv = ROPE_THETA ** (-jnp.arange(half, dtype=F32) / half)
    ang = pos.astype(F32)[:, None] * inv[None, :]
    cos = jnp.cos(ang)[None, :, None, :]
    sin = jnp.sin(ang)[None, :, None, :]
    xr = x[..., :ROT_DIM]
    x1, x2 = xr[..., :half], xr[..., half:]
    rot = jnp.concatenate([x1 * cos - x2 * sin, x2 * cos + x1 * sin], axis=-1)
    return jnp.concatenate([rot, x[..., ROT_DIM:]], axis=-1)


def causal_short_conv(x, buf, w):
    T = x.shape[1]
    xp = jnp.concatenate([buf, x], axis=1)
    y = w[0] * xp[:, 0:T]
    for j in range(1, GDN_CONV):
        y = y + w[j] * xp[:, j:j + T]
    return jax.nn.silu(y), xp[:, xp.shape[1] - (GDN_CONV - 1):]


def gated_delta_rule(q, k, v, g, beta, s0):
    B, T, H, dk = q.shape
    dv = v.shape[-1]
    C = T if T <= GDN_CHUNK else math.gcd(T, GDN_CHUNK)
    n = T // C
    q = l2_normalize(q) * (dk ** -0.5)
    k = l2_normalize(k)

    def chunks(a):
        a = a.reshape((B, n, C, H) + a.shape[3:])
        return jnp.moveaxis(a, (1, 3), (0, 2))

    qc, kc, vc, bc = chunks(q), chunks(k), chunks(v), chunks(beta)
    gc = jnp.cumsum(chunks(g), axis=-1)
    incl = jnp.tril(jnp.ones((C, C), bool))
    strict = jnp.tril(jnp.ones((C, C), bool), -1)
    decay = jnp.exp(jnp.where(incl, gc[..., :, None] - gc[..., None, :], -jnp.inf))
    kb = kc * bc[..., None]
    vb = vc * bc[..., None]
    a = jnp.where(strict, jnp.einsum('nbhid,nbhjd->nbhij', kb, kc) * decay, 0.0)
    eye = jnp.eye(C, dtype=a.dtype)
    t_inv = lax.linalg.triangular_solve(eye + a, jnp.broadcast_to(eye, a.shape), left_side=True, lower=True)
    u = t_inv @ vb
    w = t_inv @ (kb * jnp.exp(gc)[..., None])
    qk = jnp.where(incl, jnp.einsum('nbhid,nbhjd->nbhij', qc, kc) * decay, 0.0)
    q_dec = qc * jnp.exp(gc)[..., None]
    k_dec = kc * jnp.exp(gc[..., -1:] - gc)[..., None]
    g_last = jnp.exp(gc[..., -1])

    def step(s, xs):
        qk_i, u_i, w_i, q_i, k_i, gl = xs
        v_new = u_i - w_i @ s
        o = q_i @ s + qk_i @ v_new
        s = s * gl[..., None, None] + jnp.swapaxes(k_i, -1, -2) @ v_new
        return s, o

    s_final, o = lax.scan(step, s0, (qk, u, w, q_dec, k_dec, g_last))
    o = jnp.moveaxis(o, (0, 2), (1, 3)).reshape(B, T, H, dv)
    return o, s_final


def gdn_mixer(q, k, v, z, a, b, conv_buf, s0, conv_w, a_log, dt_bias, norm_w):
    B, T, _ = q.shape
    qkv, conv_new = causal_short_conv(jnp.concatenate([q, k, v], axis=-1), conv_buf, conv_w)
    q, k, v = jnp.split(qkv, [GDN_QK, 2 * GDN_QK], axis=-1)
    g = -jnp.exp(a_log) * jax.nn.softplus(a + dt_bias)
    beta = jax.nn.sigmoid(b)
    o, s_new = gated_delta_rule(q.reshape(B, T, GDN_HEADS, GDN_DK), k.reshape(B, T, GDN_HEADS, GDN_DK),
                                v.reshape(B, T, GDN_HEADS, GDN_DV), g, beta, s0)
    o = rms_norm(o, norm_w) * jax.nn.silu(z.reshape(B, T, GDN_HEADS, GDN_DV))
    return o.reshape(B, T, GDN_VW), conv_new, s_new


GDN_BLOCK = 128
GDN_HG = 2
SUBLANES = 8


def _sigmoid(x):
    return 1.0 / (1.0 + jnp.exp(-x))


SPLIT_PASSES = 3


def _dot_split(a, b):
    a_hi, b_hi = a.astype(BF16), b.astype(BF16)
    out = jnp.dot(a_hi, b_hi, preferred_element_type=F32)
    if SPLIT_PASSES == 1:
        return out
    a_lo = (a - a_hi.astype(F32)).astype(BF16)
    b_lo = (b - b_hi.astype(F32)).astype(BF16)
    return out + (jnp.dot(a_hi, b_lo, preferred_element_type=F32) + jnp.dot(a_lo, b_hi, preferred_element_type=F32))


def _gdn_prompt_kernel(q_ref, k_ref, v_ref, z_ref, cols_ref, rows_ref, wq_ref, wk_ref, wv_ref, nw_ref,
                       o_ref, sfin_ref, s_scr):
    C = GDN_BLOCK
    T = q_ref.shape[0]
    grp = pl.program_id(1)
    s_scr[...] = jnp.zeros_like(s_scr)
    ri = lax.broadcasted_iota(jnp.int32, (C, C), 0)
    ci = lax.broadcasted_iota(jnp.int32, (C, C), 1)
    incl, strict = ri >= ci, ri > ci
    eye = jnp.where(ri == ci, 1.0, 0.0)
    lane = lax.broadcasted_iota(jnp.int32, (C, LANES), 1)

    def conv_silu(ref, w_ref, i, c, r0):
        sl = slice(i * GDN_DK, (i + 1) * GDN_DK)
        prev = ref[pl.ds(pl.multiple_of(jnp.maximum(r0 - SUBLANES, 0), SUBLANES), SUBLANES), sl]
        xp = jnp.concatenate([jnp.where(c > 0, prev, 0.0), ref[pl.ds(r0, C), sl]], axis=0)
        w = w_ref[:, sl]
        y = w[GDN_CONV - 1:GDN_CONV] * xp[SUBLANES:]
        for j in range(GDN_CONV - 1):
            y = y + w[j:j + 1] * pltpu.roll(xp, GDN_CONV - 1 - j, axis=0)[SUBLANES:]
        return y * _sigmoid(y)

    def chunk(c, carry):
        r0 = pl.multiple_of(c * C, C)
        colblk = cols_ref[0, pl.ds(r0, C), :]
        for i in range(GDN_HG):
            h = grp * GDN_HG + i
            beta = jnp.sum(jnp.where(lane == h, colblk, 0.0), axis=1, keepdims=True)
            gc = jnp.sum(jnp.where(lane == GDN_HEADS + h, colblk, 0.0), axis=1, keepdims=True)
            gc_row = rows_ref[0, i, pl.ds(c, 1), :]
            g_last = gc_row[:, C - 1:C]
            q = conv_silu(q_ref, wq_ref, i, c, r0)
            k = conv_silu(k_ref, wk_ref, i, c, r0)
            v = conv_silu(v_ref, wv_ref, i, c, r0)
            q = q * lax.rsqrt(jnp.sum(q * q, axis=1, keepdims=True) + 1e-6) * (GDN_DK ** -0.5)
            k = k * lax.rsqrt(jnp.sum(k * k, axis=1, keepdims=True) + 1e-6)
            decay = jnp.exp(jnp.where(incl, gc - gc_row, -jnp.inf))
            e_gc = jnp.exp(gc)
            kb = k * beta
            k16 = k.astype(BF16)
            a = jnp.where(strict, lax.dot_general(kb.astype(BF16), k16, NT_DIMS, preferred_element_type=F32) * decay,
                          0.0)
            qk = jnp.where(incl, lax.dot_general(q.astype(BF16), k16, NT_DIMS, preferred_element_type=F32) * decay,
                           0.0)
            xp = -a
            t_inv = eye + xp
            for _ in range(int(math.log2(C)) - 1):
                xp = _dot_split(xp, xp)
                t_inv = t_inv + _dot_split(t_inv, xp)
            t16 = t_inv.astype(BF16)
            u = jnp.dot(t16, (v * beta).astype(BF16), preferred_element_type=F32)
            w = jnp.dot(t16, (kb * e_gc).astype(BF16), preferred_element_type=F32)
            s = s_scr[i]
            s16 = s.astype(BF16)
            v_new = u - jnp.dot(w.astype(BF16), s16, preferred_element_type=F32)
            v16 = v_new.astype(BF16)
            o = (jnp.dot((q * e_gc).astype(BF16), s16, preferred_element_type=F32)
                 + jnp.dot(qk.astype(BF16), v16, preferred_element_type=F32))
            k_dec = (k * jnp.exp(g_last - gc)).astype(BF16)
            s_scr[i] = s * jnp.exp(g_last) + lax.dot_general(k_dec, v16, (((0,), (0,)), ((), ())),
                                                             preferred_element_type=F32)
            z = z_ref[pl.ds(r0, C), i * GDN_DV:(i + 1) * GDN_DV]
            o = o * lax.rsqrt(jnp.mean(o * o, axis=1, keepdims=True) + 1e-6) * nw_ref[...]
            o_ref[pl.ds(r0, C), i * GDN_DV:(i + 1) * GDN_DV] = o * (z * _sigmoid(z))
        return carry

    lax.fori_loop(0, T // C, chunk, 0)
    sfin_ref[0] = s_scr[...]


def gdn_prompt(proj, a, b, conv_w, a_log, dt_bias, norm_w, B, T):
    C, HG, H = GDN_BLOCK, GDN_HG, GDN_HEADS
    assert T % C == 0 and H % HG == 0 and 2 * H <= LANES
    g = -jnp.exp(a_log) * jax.nn.softplus(a + dt_bias)
    beta = jax.nn.sigmoid(b)
    gc = jnp.cumsum(g.reshape(B, T // C, C, H), axis=2)
    cols = jnp.concatenate([beta, gc.reshape(B, T, H), jnp.zeros((B, T, LANES - 2 * H), F32)], axis=-1)
    rows = jnp.moveaxis(gc, 3, 1)
    wblk = HG * GDN_DK
    nq = GDN_QK // wblk
    wide = lambda off: pl.BlockSpec((T, wblk), lambda bi, gi: (bi, off * nq + gi))
    taps = lambda off: pl.BlockSpec((GDN_CONV, wblk), lambda bi, gi: (0, off * nq + gi))
    return pl.pallas_call(
        _gdn_prompt_kernel,
        grid=(B, H // HG),
        in_specs=[wide(0), wide(1), wide(2), wide(3),
                  pl.BlockSpec((1, T, LANES), lambda bi, gi: (bi, 0, 0)),
                  pl.BlockSpec((1, HG, T // C, C), lambda bi, gi: (bi, gi, 0, 0)),
                  taps(0), taps(1), taps(2),
                  pl.BlockSpec((1, GDN_DV), lambda bi, gi: (0, 0))],
        out_specs=[pl.BlockSpec((T, wblk), lambda bi, gi: (bi, gi)),
                   pl.BlockSpec((1, HG, GDN_DK, GDN_DV), lambda bi, gi: (bi, gi, 0, 0))],
        out_shape=[jax.ShapeDtypeStruct((B * T, GDN_VW), F32), jax.ShapeDtypeStruct((B, H, GDN_DK, GDN_DV), F32)],
        scratch_shapes=[pltpu.VMEM((HG, GDN_DK, GDN_DV), F32)],
        compiler_params=pltpu.CompilerParams(dimension_semantics=("parallel", "parallel"),
                                             vmem_limit_bytes=VMEM_LIMIT),
        name="gdn_prompt",
    )(proj, proj, proj, proj, cols, rows, conv_w, conv_w, conv_w, norm_w.reshape(1, GDN_DV))


def compress_blocks(x, pos_emb, w1, w2):
    B, Tk, G, d = x.shape
    nsub = Tk // CMP_STRIDE
    sub = x[:, :nsub * CMP_STRIDE].reshape(B, nsub, CMP_STRIDE, G, d)
    blocks = jnp.concatenate([sub[:, :-1], sub[:, 1:]], axis=2) + pos_emb[:, None, :]
    h = jax.nn.gelu(jnp.einsum('bnjgd,jde->bnge', blocks, w1.reshape(CMP_LEN, d, CMP_HIDDEN)))
    return jnp.einsum('bnge,ef->bngf', h, w2)


def selection_weights(n_cmp, n_slc):
    r = SLC_BLOCK // CMP_STRIDE
    s = CMP_LEN // CMP_STRIDE
    wts = np.convolve(np.ones(r), np.ones(s)).astype(np.float32)
    off = np.arange(n_cmp)[:, None] - r * np.arange(n_slc)[None, :]
    return np.where((off >= 0) & (off < wts.size), wts[np.clip(off, 0, wts.size - 1)], 0.0).astype(np.float32)


PAGE_SIZE = 128
N_PAGES = PAST_LEN // PAGE_SIZE
PAGE_ROWS = PAGE_SIZE * NSA_KV_HEADS
SUB_ROWS = CMP_STRIDE * NSA_KV_HEADS
SUBS_PER_PAGE = PAGE_SIZE // CMP_STRIDE
N_SUB = PAST_LEN // CMP_STRIDE
DEC_SEQ = 8
Q_ROWS = DEC_SEQ * NSA_HEADS
NEW_ROWS = DEC_SEQ * NSA_KV_HEADS


def _sample_compress_kernel(pt_ref, *refs):
    k_pages, v_pages = refs[:N_PAGES], refs[N_PAGES:2 * N_PAGES]
    pos_k, w1_k, w2_k, pos_v, w1_v, w2_v, ok_ref, ov_ref = refs[2 * N_PAGES:]
    G = NSA_KV_HEADS

    def compress(pages, pos_ref, w1_ref, w2_ref, out_ref):
        first = jnp.zeros((G * N_SUB, CMP_HIDDEN), F32)
        second = jnp.zeros((G * N_SUB, CMP_HIDDEN), F32)
        for j in range(CMP_STRIDE):
            x = jnp.concatenate([pg[0, pl.ds(j * G + g, SUBS_PER_PAGE, stride=SUB_ROWS), :]
                                 for g in range(G) for pg in pages], axis=0)
            lo = (x + pos_ref[j:j + 1, :]).astype(BF16)
            hi = (x + pos_ref[CMP_STRIDE + j:CMP_STRIDE + j + 1, :]).astype(BF16)
            first = first + jnp.dot(lo, w1_ref[j * NSA_DIM:(j + 1) * NSA_DIM, :], preferred_element_type=F32)
            second = second + jnp.dot(hi, w1_ref[(CMP_STRIDE + j) * NSA_DIM:(CMP_STRIDE + j + 1) * NSA_DIM, :],
                                      preferred_element_type=F32)
        h = first + pltpu.roll(second, G * N_SUB - 1, axis=0)
        out_ref[0] = jnp.dot(_gelu_tanh(h).astype(BF16), w2_ref[...], preferred_element_type=F32)

    compress(k_pages, pos_k, w1_k, w2_k, ok_ref)
    compress(v_pages, pos_v, w1_v, w2_v, ov_ref)


def _paged_specs(n):
    return [pl.BlockSpec((1, PAGE_ROWS, NSA_DIM), lambda b, pt, p=p: (pt[b, p], 0, 0)) for p in range(N_PAGES)] * n


def sample_compress(page_table, cmp_k, cmp_v, pos_k, w1_k, w2_k, pos_v, w1_v, w2_v):
    db = page_table.shape[0]
    full = lambda shape: pl.BlockSpec(shape, lambda b, pt: (0,) * len(shape))
    out = pl.BlockSpec((1, NSA_KV_HEADS * N_SUB, NSA_DIM), lambda b, pt: (b, 0, 0))
    return pl.pallas_call(
        _sample_compress_kernel,
        grid_spec=pltpu.PrefetchScalarGridSpec(
            num_scalar_prefetch=1, grid=(db,),
            in_specs=_paged_specs(1) + _paged_specs(1)
            + [full((CMP_LEN, NSA_DIM)), full((CMP_LEN * NSA_DIM, CMP_HIDDEN)), full((CMP_HIDDEN, NSA_DIM))] * 2,
            out_specs=[out, out]),
        out_shape=[jax.ShapeDtypeStruct((db, NSA_KV_HEADS * N_SUB, NSA_DIM), F32)] * 2,
        compiler_params=pltpu.CompilerParams(dimension_semantics=("parallel",), vmem_limit_bytes=VMEM_LIMIT),
        name="sample_compress",
    )(page_table, *([cmp_k] * N_PAGES), *([cmp_v] * N_PAGES), pos_k, w1_k.astype(BF16), w2_k.astype(BF16),
      pos_v, w1_v.astype(BF16), w2_v.astype(BF16))


def _sample_attention_kernel(pt_ref, *refs):
    k_pages, v_pages = refs[:N_PAGES], refs[N_PAGES:2 * N_PAGES]
    (wk_ref, wv_ref, kc_ref, vc_ref, q_ref, qr_ref, ksn_ref, vsn_ref, kwn_ref, vwn_ref, m_ref,
     o_ref, s_scr) = refs[2 * N_PAGES:]
    G, R = NSA_KV_HEADS, NSA_GROUP
    scale = NSA_DIM ** -0.5
    n_slc = -(-(PAST_LEN + DEC_SEQ) // SLC_BLOCK)
    blocks_per_page = PAGE_SIZE // SLC_BLOCK

    def iota(shape, axis):
        return lax.broadcasted_iota(jnp.int32, shape, axis)

    def q_side(width):
        r = iota((Q_ROWS, width), 0)
        return r // NSA_HEADS, (r % NSA_HEADS) // R

    def scores(q16, keys):
        return lax.dot_general(q16, keys.astype(BF16), NT_DIMS, preferred_element_type=F32) * scale

    def pad_new(ref):
        return jnp.concatenate([ref[...], jnp.zeros((LANES - NEW_ROWS, NSA_DIM), F32)], axis=0).astype(BF16)

    q16, qr16 = q_ref[...].astype(BF16), qr_ref[...].astype(BF16)

    tq, gq = q_side(G * N_SUB)
    c = iota((Q_ROWS, G * N_SUB), 1)
    ok = (c // N_SUB == gq) & (c % N_SUB < N_SUB - 1)
    s = jnp.where(ok, scores(q16, kc_ref[0]), NEG)
    e = jnp.exp(s - jnp.max(s, axis=1, keepdims=True))
    p = jnp.where(ok, e / jnp.sum(e, axis=1, keepdims=True), 0.0)
    p16 = p.astype(BF16)
    o_ref[0] = jnp.dot(p16, vc_ref[0].astype(BF16), preferred_element_type=F32)

    imp_head = jnp.dot(p16, m_ref[...], preferred_element_type=F32)
    tq, gq = q_side(Q_ROWS)
    cq = iota((Q_ROWS, Q_ROWS), 1)
    same_group = jnp.where((cq // NSA_HEADS == tq) & ((cq % NSA_HEADS) // R == gq), 1.0, 0.0)
    imp = _dot_split(same_group, imp_head)
    tq, gq = q_side(LANES)
    blk = iota((Q_ROWS, LANES), 1)
    pos = PAST_LEN + tq
    cur = pos // SLC_BLOCK
    valid = (blk * SLC_BLOCK <= pos) & (blk < n_slc)
    forced = (blk == 0) | (blk == cur) | (blk == cur - 1)
    score = jnp.where(valid, jnp.where(forced, FORCE, imp), NEG)
    rank = jnp.zeros((Q_ROWS, LANES), F32)
    for i in range(n_slc):
        si = score[:, i:i + 1]
        rank = rank + jnp.where((si > score) | ((si == score) & (i < blk)), 1.0, 0.0)
    sel = jnp.where((rank < SLC_TOPK) & valid, 1.0, 0.0)

    def new_mask():
        tq, gq = q_side(LANES)
        c = iota((Q_ROWS, LANES), 1)
        return (c < NEW_ROWS) & (c % G == gq) & (c // G <= tq)

    tq, gq = q_side(PAGE_ROWS)
    c = iota((Q_ROWS, PAGE_ROWS), 1)
    own_head = c % G == gq
    first_block = c // G < SLC_BLOCK
    for pg in range(N_PAGES):
        picked = jnp.where(first_block, sel[:, blocks_per_page * pg:blocks_per_page * pg + 1],
                           sel[:, blocks_per_page * pg + 1:blocks_per_page * pg + 2])
        s_scr[:, pg * PAGE_ROWS:(pg + 1) * PAGE_ROWS] = jnp.where(own_head & (picked > 0.5),
                                                                   scores(qr16, k_pages[pg][0]), NEG)
    past_cols = N_PAGES * PAGE_ROWS
    new_block = PAST_LEN // SLC_BLOCK
    s_scr[:, past_cols:past_cols + LANES] = jnp.where(new_mask() & (sel[:, new_block:new_block + 1] > 0.5),
                                                      scores(qr16, pad_new(ksn_ref)), NEG)
    s = s_scr[...]
    e = jnp.exp(s - jnp.max(s, axis=1, keepdims=True))
    acc = jnp.dot(e[:, past_cols:].astype(BF16), pad_new(vsn_ref), preferred_element_type=F32)
    for pg in range(N_PAGES):
        acc = acc + jnp.dot(e[:, pg * PAGE_ROWS:(pg + 1) * PAGE_ROWS].astype(BF16), v_pages[pg][0].astype(BF16),
                            preferred_element_type=F32)
    o_ref[1] = acc / jnp.sum(e, axis=1, keepdims=True)

    wrows = WINDOW * G
    tq, gq = q_side(wrows)
    c = iota((Q_ROWS, wrows), 1)
    s_buf = jnp.where((c % G == gq) & (c // G > tq), scores(qr16, wk_ref[0]), NEG)
    s_new = jnp.where(new_mask(), scores(qr16, pad_new(kwn_ref)), NEG)
    m = jnp.maximum(jnp.max(s_buf, axis=1, keepdims=True), jnp.max(s_new, axis=1, keepdims=True))
    e_buf, e_new = jnp.exp(s_buf - m), jnp.exp(s_new - m)
    acc = (jnp.dot(e_buf.astype(BF16), wv_ref[0].astype(BF16), preferred_element_type=F32)
           + jnp.dot(e_new.astype(BF16), pad_new(vwn_ref), preferred_element_type=F32))
    o_ref[2] = acc / (jnp.sum(e_buf, axis=1, keepdims=True) + jnp.sum(e_new, axis=1, keepdims=True))


def sample_attention(page_table, slc_k, slc_v, win_k, win_v, k_cmp, v_cmp, q, q_rot, ks_new, vs_new, kw_new, vw_new):
    db = page_table.shape[0]
    assert WINDOW == win_k.shape[1] // NSA_KV_HEADS and q.shape[0] == db * Q_ROWS
    n_cmp = N_SUB - 1
    n_slc = -(-(PAST_LEN + DEC_SEQ) // SLC_BLOCK)
    sel_m = np.zeros((NSA_KV_HEADS, N_SUB, LANES), np.float32)
    sel_m[:, :n_cmp, :n_slc] = selection_weights(n_cmp, n_slc)[None]
    sel_m = jnp.asarray(sel_m.reshape(NSA_KV_HEADS * N_SUB, LANES), BF16)
    per_seq = lambda rows: pl.BlockSpec((1, rows, NSA_DIM), lambda b, pt: (b, 0, 0))
    flat = lambda rows: pl.BlockSpec((rows, NSA_DIM), lambda b, pt: (b, 0))
    return pl.pallas_call(
        _sample_attention_kernel,
        grid_spec=pltpu.PrefetchScalarGridSpec(
            num_scalar_prefetch=1, grid=(db,),
            in_specs=_paged_specs(1) + _paged_specs(1)
            + [per_seq(WINDOW * NSA_KV_HEADS)] * 2 + [per_seq(NSA_KV_HEADS * N_SUB)] * 2
            + [flat(Q_ROWS)] * 2 + [flat(NEW_ROWS)] * 4
            + [pl.BlockSpec((NSA_KV_HEADS * N_SUB, LANES), lambda b, pt: (0, 0))],
            out_specs=pl.BlockSpec((3, Q_ROWS, NSA_DIM), lambda b, pt: (0, b, 0)),
            scratch_shapes=[pltpu.VMEM((Q_ROWS, N_PAGES * PAGE_ROWS + LANES), F32)]),
        out_shape=jax.ShapeDtypeStruct((3, db * Q_ROWS, NSA_DIM), F32),
        compiler_params=pltpu.CompilerParams(dimension_semantics=("parallel",), vmem_limit_bytes=VMEM_LIMIT),
        name="sample_attention",
    )(page_table, *([slc_k] * N_PAGES), *([slc_v] * N_PAGES), win_k, win_v, k_cmp, v_cmp, q, q_rot,
      ks_new, vs_new, kw_new, vw_new, sel_m)


PROMPT_TQ = 128


def _dot_split_rhs(a16, b):
    b_hi = b.astype(BF16)
    b_lo = (b - b_hi.astype(F32)).astype(BF16)
    return jnp.dot(a16, b_hi, preferred_element_type=F32) + jnp.dot(a16, b_lo, preferred_element_type=F32)


def _prompt_attention_kernel(q_ref, qr_ref, kc_ref, vc_ref, ks_ref, vs_ref, kw_ref, vw_ref, m_ref, x_ref, o_ref):
    R, TQ = NSA_GROUP, PROMPT_TQ
    QR = TQ * R
    T = ks_ref.shape[2]
    n_cmp = T // CMP_STRIDE - 1
    n_slc = -(-T // SLC_BLOCK)
    scale = NSA_DIM ** -0.5
    i = pl.program_id(2)

    def iota(shape, axis):
        return lax.broadcasted_iota(jnp.int32, shape, axis)

    def q_pos(width):
        return i * TQ + iota((QR, width), 0) // R

    def scores(q16, keys):
        return lax.dot_general(q16, keys.astype(BF16), NT_DIMS, preferred_element_type=F32) * scale

    q16, qr16 = q_ref[0, 0].astype(BF16), qr_ref[0, 0].astype(BF16)

    n = iota((QR, LANES), 1)
    pos = q_pos(LANES)
    ok = (n * CMP_STRIDE + (CMP_LEN - 1) <= pos) & (n < n_cmp)
    s = jnp.where(ok, scores(q16, kc_ref[0, 0]), NEG)
    e = jnp.exp(s - jnp.max(s, axis=1, keepdims=True))
    p16 = jnp.where(ok, e / jnp.sum(e, axis=1, keepdims=True), 0.0).astype(BF16)
    o_ref[0, 0, 0] = jnp.dot(p16, vc_ref[0, 0].astype(BF16), preferred_element_type=F32)

    imp_head = jnp.dot(p16, m_ref[...], preferred_element_type=F32)
    rows_of_token = jnp.where(iota((TQ, QR), 1) // R == iota((TQ, QR), 0), 1.0, 0.0).astype(BF16)
    imp = _dot_split_rhs(rows_of_token, imp_head)
    blk = iota((TQ, LANES), 1)
    pos = i * TQ + iota((TQ, LANES), 0)
    cur = pos // SLC_BLOCK
    valid = (blk * SLC_BLOCK <= pos) & (blk < n_slc)
    forced = (blk == 0) | (blk == cur) | (blk == cur - 1)
    score = jnp.where(valid, jnp.where(forced, FORCE, imp), NEG)
    rank = jnp.zeros((TQ, LANES), F32)
    for b in range(n_slc):
        sb = score[:, b:b + 1]
        rank = rank + jnp.where((sb > score) | ((sb == score) & (b < blk)), 1.0, 0.0)
    sel16 = jnp.where((rank < min(SLC_TOPK, n_slc)) & valid, 1.0, 0.0).astype(BF16)

    key_sel_tok = jnp.dot(sel16, x_ref[...], preferred_element_type=F32).astype(BF16)
    token_of_row = jnp.where(iota((QR, TQ), 0) // R == iota((QR, TQ), 1), 1.0, 0.0).astype(BF16)
    key_sel = jnp.dot(token_of_row, key_sel_tok, preferred_element_type=F32)
    ok = (key_sel > 0.5) & (iota((QR, T), 1) <= q_pos(T))
    s = jnp.where(ok, scores(qr16, ks_ref[0, 0]), NEG)
    e = jnp.exp(s - jnp.max(s, axis=1, keepdims=True))
    o_ref[1, 0, 0] = (jnp.dot(e.astype(BF16), vs_ref[0, 0].astype(BF16), preferred_element_type=F32)
                      / jnp.sum(e, axis=1, keepdims=True))

    span = WINDOW + TQ
    start = pl.multiple_of(jnp.maximum(i - WINDOW // TQ, 0) * TQ, TQ)
    k_pos = start + iota((QR, span), 1)
    pos = q_pos(span)
    ok = (k_pos <= pos) & (k_pos > pos - WINDOW)
    s = jnp.where(ok, scores(qr16, kw_ref[0, 0, pl.ds(start, span), :]), NEG)
    e = jnp.exp(s - jnp.max(s, axis=1, keepdims=True))
    o_ref[2, 0, 0] = (jnp.dot(e.astype(BF16), vw_ref[0, 0, pl.ds(start, span), :].astype(BF16),
                              preferred_element_type=F32) / jnp.sum(e, axis=1, keepdims=True))


def prompt_attention(q, q_rot, k_cmp, v_cmp, ks, vs, kw, vw):
    B, T, H, d = q.shape
    G, R, TQ = NSA_KV_HEADS, NSA_GROUP, PROMPT_TQ
    n_cmp = k_cmp.shape[1]
    n_slc = -(-T // SLC_BLOCK)
    assert T % TQ == 0 and n_cmp == T // CMP_STRIDE - 1 and n_cmp <= LANES and n_slc <= LANES and T >= WINDOW + TQ
    by_head = lambda a: jnp.swapaxes(a, 1, 2)
    q_rows = lambda a: jnp.swapaxes(a.reshape(B, T, G, R, d), 1, 2).reshape(B, G, T * R, d)
    pad_cmp = lambda a: jnp.pad(by_head(a), ((0, 0), (0, 0), (0, LANES - n_cmp), (0, 0)))
    sel_m = np.zeros((LANES, LANES), np.float32)
    sel_m[:n_cmp, :n_slc] = selection_weights(n_cmp, n_slc)
    expand = (np.arange(T)[None, :] // SLC_BLOCK == np.arange(LANES)[:, None]).astype(np.float32)
    rows = pl.BlockSpec((1, 1, TQ * R, d), lambda b, g, i: (b, g, i, 0))
    seq = lambda n: pl.BlockSpec((1, 1, n, d), lambda b, g, i: (b, g, 0, 0))
    const = lambda shape: pl.BlockSpec(shape, lambda b, g, i: (0, 0))
    o = pl.pallas_call(
        _prompt_attention_kernel,
        grid=(B, G, T // TQ),
        in_specs=[rows, rows, seq(LANES), seq(LANES), seq(T), seq(T), seq(T), seq(T),
                  const((LANES, LANES)), const((LANES, T))],
        out_specs=pl.BlockSpec((3, 1, 1, TQ * R, d), lambda b, g, i: (0, b, g, i, 0)),
        out_shape=jax.ShapeDtypeStruct((3, B, G, T * R, d), F32),
        compiler_params=pltpu.CompilerParams(dimension_semantics=("parallel", "parallel", "parallel"),
                                             vmem_limit_bytes=VMEM_LIMIT),
        name="prompt_attention",
    )(q_rows(q), q_rows(q_rot), pad_cmp(k_cmp), pad_cmp(v_cmp), by_head(ks), by_head(vs), by_head(kw), by_head(vw),
      jnp.asarray(sel_m, BF16), jnp.asarray(expand, BF16))
    return jnp.swapaxes(o.reshape(3, B, G, T, R, d), 2, 3).reshape(3, B, T, H, d)


ROUTE_TOKENS = LANES


def _top_rows(s, row_id, k, payload=None):
    slot = lax.broadcasted_iota(jnp.int32, (k, s.shape[1]), 0)
    vals = jnp.zeros((k, s.shape[1]), F32)
    picks = jnp.zeros((k, s.shape[1]), F32)
    for t in range(k):
        m = jnp.max(s, axis=0, keepdims=True)
        i = jnp.min(jnp.where(s == m, row_id, jnp.inf), axis=0, keepdims=True)
        hit = row_id == i
        pick = i if payload is None else jnp.sum(jnp.where(hit, payload, 0.0), axis=0, keepdims=True)
        vals = jnp.where(slot == t, m, vals)
        picks = jnp.where(slot == t, pick, picks)
        s = jnp.where(hit, -jnp.inf, s)
    return vals, picks


def _peer_route_kernel(q_ref, keys_ref, gate_ref, e_scr, g_scr):
    T = ROUTE_TOKENS
    half = PEER_QDIM // 2
    key_id = lax.broadcasted_iota(jnp.int32, (PEER_KEYS, T), 0).astype(F32)
    lo = PEER_TOPK // 2
    cand_row = lax.broadcasted_iota(jnp.int32, (lo * PEER_TOPK + lo, T), 0)
    cand_id = jnp.where(cand_row < lo * PEER_TOPK, cand_row,
                        lo * PEER_TOPK + (cand_row - lo * PEER_TOPK) * PEER_TOPK).astype(F32)
    experts, gates = [], []
    for h in range(PEER_HEADS):
        sub = []
        for p in range(2):
            q = q_ref[:, (2 * h + p) * half:(2 * h + p + 1) * half].astype(BF16)
            s = lax.dot_general(keys_ref[2 * h + p], q, NT_DIMS, preferred_element_type=F32)
            sub.append(_top_rows(s, key_id, PEER_TOPK))
        (s1, i1), (s2, i2) = sub
        cand = jnp.concatenate([s1[k:k + 1] + s2 for k in range(lo)] + [s1[lo:] + s2[0:1]], axis=0)
        cand_e = jnp.concatenate([i1[k:k + 1] * float(PEER_KEYS) + i2 for k in range(lo)]
                                 + [i1[lo:] * float(PEER_KEYS) + i2[0:1]], axis=0)
        top_s, top_e = _top_rows(cand, cand_id, PEER_TOPK, payload=cand_e)
        ex = jnp.exp(top_s - top_s[0:1])
        experts.append(top_e)
        gates.append(ex / jnp.sum(ex, axis=0, keepdims=True))
    e_scr[...] = jnp.concatenate(experts, axis=0).T
    g_scr[...] = jnp.concatenate(gates, axis=0).T

    sub_id = lax.broadcasted_iota(jnp.int32, (PEER_KEYS, PEER_HEADS * PEER_TOPK), 0).astype(F32)

    def scatter_token(n, carry):
        e = e_scr[pl.ds(n, 1), :]
        a = jnp.floor(e * (1.0 / PEER_KEYS))
        b = e - a * float(PEER_KEYS)
        lhs = jnp.where(sub_id == a, g_scr[pl.ds(n, 1), :], 0.0)
        lhs_hi = lhs.astype(BF16)
        lhs_lo = (lhs - lhs_hi.astype(F32)).astype(BF16)
        rhs = jnp.where(sub_id == b, 1.0, 0.0).astype(BF16)
        gate_ref[n] = (lax.dot_general(lhs_hi, rhs, NT_DIMS, preferred_element_type=F32)
                       + lax.dot_general(lhs_lo, rhs, NT_DIMS, preferred_element_type=F32))
        return carry

    lax.fori_loop(0, T, scatter_token, 0, unroll=8)


def peer_routing(q, sub_keys):
    n = q.shape[0]
    assert n % ROUTE_TOKENS == 0
    keys = sub_keys.reshape(PEER_HEADS * 2, PEER_KEYS, PEER_QDIM // 2).astype(BF16)
    j = PEER_HEADS * PEER_TOPK
    return pl.pallas_call(
        _peer_route_kernel,
        grid=(n // ROUTE_TOKENS,),
        in_specs=[pl.BlockSpec((ROUTE_TOKENS, q.shape[1]), lambda i: (i, 0)),
                  pl.BlockSpec(keys.shape, lambda i: (0, 0, 0))],
        out_specs=pl.BlockSpec((ROUTE_TOKENS, PEER_KEYS, PEER_KEYS), lambda i: (i, 0, 0)),
        out_shape=jax.ShapeDtypeStruct((n, PEER_KEYS, PEER_KEYS), F32),
        scratch_shapes=[pltpu.VMEM((ROUTE_TOKENS, j), F32), pltpu.VMEM((ROUTE_TOKENS, j), F32)],
        compiler_params=pltpu.CompilerParams(dimension_semantics=("parallel",), vmem_limit_bytes=VMEM_LIMIT),
        name="peer_routing",
    )(q, keys)


def _permuted_in_weight(w_in):
    cuts = np.cumsum(IN_SIZES)[:-1].tolist()
    parts = jnp.split(w_in, cuts, axis=-1)
    wide = [parts[i] for i in (0, 1, 2, 3, 6, 7, 8, 9, 10, 11, 12)]
    narrow = [parts[i] for i in (4, 5, 13)]
    n_narrow = sum(p.shape[1] for p in narrow)
    pad = jnp.zeros((w_in.shape[0], LANES - n_narrow), w_in.dtype)
    return jnp.concatenate(wide, axis=-1).astype(BF16), jnp.concatenate(narrow + [pad], axis=-1).astype(BF16)


def _split_projection(wide, narrow, B, T):
    sizes = [IN_SIZES[i] for i in (0, 1, 2, 3, 6, 7, 8, 9, 10, 11, 12)]
    cuts = np.cumsum(sizes)[:-1].tolist()
    gq, gk, gv, gz, nq, kc, vc, ks, vs, kw, vw = [p.reshape(B, T, -1) for p in jnp.split(wide, cuts, axis=-1)]
    ga = narrow[:, :GDN_HEADS].reshape(B, T, -1)
    gb = narrow[:, GDN_HEADS:2 * GDN_HEADS].reshape(B, T, -1)
    ngate = narrow[:, 2 * GDN_HEADS:2 * GDN_HEADS + 3 * NSA_HEADS].reshape(B, T, -1)
    return gq, gk, gv, gz, ga, gb, nq, kc, vc, ks, vs, kw, vw, ngate


def _prompt_group(proj, B, T, pos, gdn, weights):
    (conv_w, a_log, dt_bias, norm_w, pos_k, w1_k, w2_k, pos_v, w1_v, w2_v) = weights
    H, G, hd = NSA_HEADS, NSA_KV_HEADS, NSA_DIM
    (gq, gk, gv, gz, ga, gb, nq, kc, vc, ks, vs, kw, vw, ngate) = proj
    o_gdn, conv_new, ssm_new = gdn(gq, gk, gv, gz, ga, gb)
    q = nq.reshape(B, T, H, hd)
    q_rot = partial_rope(q, pos)
    kc = kc.reshape(B, T, G, hd)
    vc = vc.reshape(B, T, G, hd)
    ks = partial_rope(ks.reshape(B, T, G, hd), pos)
    vs = vs.reshape(B, T, G, hd)
    kw = partial_rope(kw.reshape(B, T, G, hd), pos)
    vw = vw.reshape(B, T, G, hd)
    k_cmp = compress_blocks(kc, pos_k, w1_k, w2_k)
    v_cmp = compress_blocks(vc, pos_v, w1_v, w2_v)
    o3 = prompt_attention(q, q_rot, k_cmp, v_cmp, ks, vs, kw, vw)
    keep = min(WINDOW, T)
    win_k_new, win_v_new = kw[:, T - keep:], vw[:, T - keep:]
    gate = jax.nn.sigmoid(ngate).reshape(B, T, 3, H, 1)
    o_nsa = gate[:, :, 0] * o3[0] + gate[:, :, 1] * o3[1] + gate[:, :, 2] * o3[2]
    mix_in = jnp.concatenate([o_gdn, o_nsa.reshape(B, T, NSA_QW)], axis=-1)
    return mix_in.reshape(B * T, D_MODEL), (kc, vc, ks, vs, win_k_new, win_v_new, conv_new, ssm_new)


def _sample_group(proj, DB, L, pos, page_table, pools, win_buf, gdn_state, weights):
    (conv_w, a_log, dt_bias, norm_w, pos_k, w1_k, w2_k, pos_v, w1_v, w2_v) = weights
    H, G, hd = NSA_HEADS, NSA_KV_HEADS, NSA_DIM
    assert L == DEC_SEQ and page_table.shape[1] == N_PAGES and pools[0].shape[1] == PAGE_SIZE
    (gq, gk, gv, gz, ga, gb, nq, kc, vc, ks, vs, kw, vw, ngate) = proj
    conv_buf, ssm0 = gdn_state
    o_gdn, conv_new, ssm_new = gdn_mixer(gq, gk, gv, gz, ga, gb, conv_buf, ssm0, conv_w, a_log, dt_bias, norm_w)
    q = nq.reshape(DB, L, H, hd)
    q_rot = partial_rope(q, pos)
    kc = kc.reshape(DB, L, G, hd)
    vc = vc.reshape(DB, L, G, hd)
    ks = partial_rope(ks.reshape(DB, L, G, hd), pos)
    vs = vs.reshape(DB, L, G, hd)
    kw = partial_rope(kw.reshape(DB, L, G, hd), pos)
    vw = vw.reshape(DB, L, G, hd)
    rows = lambda pool: pool.reshape(pool.shape[0], PAGE_ROWS, hd)
    cmp_k, cmp_v, slc_k, slc_v = [rows(p) for p in pools]
    k_buf, v_buf = win_buf
    k_cmp, v_cmp = sample_compress(page_table, cmp_k, cmp_v, pos_k, w1_k, w2_k, pos_v, w1_v, w2_v)
    new_rows = lambda a: a.reshape(DB * L * G, hd)
    o3 = sample_attention(page_table, slc_k, slc_v, k_buf.reshape(DB, WINDOW * G, hd), v_buf.reshape(DB, WINDOW * G, hd),
                          k_cmp, v_cmp, q.reshape(DB * L * H, hd), q_rot.reshape(DB * L * H, hd),
                          new_rows(ks), new_rows(vs), new_rows(kw), new_rows(vw)).reshape(3, DB, L, H, hd)
    gate = jax.nn.sigmoid(ngate).reshape(DB, L, 3, H, 1)
    o_nsa = gate[:, :, 0] * o3[0] + gate[:, :, 1] * o3[1] + gate[:, :, 2] * o3[2]
    win_k_new = jnp.concatenate([k_buf, kw], axis=1)[:, L:]
    win_v_new = jnp.concatenate([v_buf, vw], axis=1)[:, L:]
    mix_in = jnp.concatenate([o_gdn, o_nsa.reshape(DB, L, NSA_QW)], axis=-1)
    return mix_in.reshape(DB * L, D_MODEL), (kc, vc, ks, vs, win_k_new, win_v_new, conv_new, ssm_new)


def kernel(x_prompt, x_sample, cache_cmp_k, cache_cmp_v, cache_slc_k, cache_slc_v, cache_win_k, cache_win_v,
           state_conv, state_ssm, page_table, w_in, gdn_conv_w, gdn_a_log, gdn_dt_bias, gdn_norm_w,
           cmp_pos_k, cmp_w1_k, cmp_w2_k, cmp_pos_v, cmp_w1_v, cmp_w2_v, w_out, ln1_g, ln1_b,
           peer_w_query, peer_sub_keys, peer_u, peer_v, ln2_g, ln2_b):
    assert w_in.shape[0] == DEPTH == 1
    G, hd = NSA_KV_HEADS, NSA_DIM
    B, T, _ = x_prompt.shape
    DB, L, _ = x_sample.shape
    NP, NS = B * T, DB * L
    pos_p = jnp.arange(T, dtype=jnp.int32)
    pos_s = PAST_LEN + jnp.arange(L, dtype=jnp.int32)
    l = 0
    x_all = jnp.concatenate([x_prompt.reshape(NP, D_MODEL), x_sample.reshape(NS, D_MODEL)], axis=0)

    w_wide, w_narrow = _permuted_in_weight(w_in[l])
    x_bf = x_all.astype(BF16)
    proj_wide = matmul(x_bf, w_wide, 512, 1024, "in_proj_wide")
    proj_narrow = matmul(x_bf, w_narrow, 512, LANES, "in_proj_narrow")

    weights = (gdn_conv_w[l], gdn_a_log[l], gdn_dt_bias[l], gdn_norm_w[l], cmp_pos_k[l], cmp_w1_k[l], cmp_w2_k[l],
               cmp_pos_v[l], cmp_w1_v[l], cmp_w2_v[l])
    def gdn_prompt_group(gq, gk, gv, gz, ga, gb):
        o, ssm = gdn_prompt(proj_wide, ga, gb, gdn_conv_w[l], gdn_a_log[l], gdn_dt_bias[l], gdn_norm_w[l], B, T)
        conv_new = jnp.concatenate([gq, gk, gv], axis=-1)[:, T - (GDN_CONV - 1):]
        return o.reshape(B, T, GDN_VW), conv_new, ssm

    mix_p, new_p = _prompt_group(_split_projection(proj_wide[:NP], proj_narrow[:NP], B, T), B, T, pos_p,
                                 gdn_prompt_group, weights)

    mix_s, new_s = _sample_group(_split_projection(proj_wide[NP:], proj_narrow[NP:], DB, L), DB, L, pos_s, page_table,
                                 (cache_cmp_k[l], cache_cmp_v[l], cache_slc_k[l], cache_slc_v[l]),
                                 (cache_win_k[l], cache_win_v[l]), (state_conv[l], state_ssm[l]), weights)

    mix_in = jnp.concatenate([mix_p, mix_s], axis=0).astype(BF16)
    mixed = matmul(mix_in, w_out[l].astype(BF16), 512, 1024, "out_proj")
    x1 = layer_norm(DN_ALPHA * x_all + mixed, ln1_g[l], ln1_b[l])

    x1_bf = x1.astype(BF16)
    pq = matmul(x1_bf, peer_w_query[l].astype(BF16), 512, 1024, "peer_query")
    gate = peer_routing(pq, peer_sub_keys[l])
    y = peer_experts(x1_bf, peer_u[l].astype(BF16), peer_v[l].astype(BF16), gate, 512, 1024)
    x2 = layer_norm(DN_ALPHA * x1 + y, ln2_g[l], ln2_b[l])

    outs_p = [a[None] for a in new_p]
    outs_s = [a[None] for a in new_s]
    return (x2[:NP].reshape(B, T, D_MODEL), x2[NP:].reshape(DB, L, D_MODEL), *outs_p, *outs_s)
```

```python
import functools
import math

import jax
import jax.numpy as jnp
import numpy as np
from jax import lax
from jax.experimental import pallas as pl
from jax.experimental.pallas import tpu as pltpu

D_MODEL = 4096
PAST_LEN = 2048
GDN_DK = 128
GDN_DV = 128
GDN_HEADS = D_MODEL // 2 // GDN_DV
GDN_CONV = 4
GDN_CHUNK = 64
GDN_QK = GDN_HEADS * GDN_DK
GDN_VW = GDN_HEADS * GDN_DV
NSA_DIM = 128
NSA_HEADS = (D_MODEL - GDN_VW) // NSA_DIM
NSA_GROUP = 4
NSA_KV_HEADS = NSA_HEADS // NSA_GROUP
NSA_QW = NSA_HEADS * NSA_DIM
NSA_KVW = NSA_KV_HEADS * NSA_DIM
CMP_STRIDE = 16
CMP_LEN = 2 * CMP_STRIDE
CMP_HIDDEN = NSA_DIM
SLC_BLOCK = 64
SLC_TOPK = 16
WINDOW = 512
ROT_DIM = NSA_DIM // 4
ROPE_THETA = 500000.0
PEER_HEADS = 8
PEER_KEYS = 128
PEER_EXPERTS = PEER_KEYS * PEER_KEYS
PEER_TOPK = 16
PEER_QDIM = 256
DEPTH = 1
DN_ALPHA = (2.0 * DEPTH) ** 0.25
LN_EPS = 1e-5
NEG = -1e30
FORCE = 1e30
IN_SIZES = (GDN_QK, GDN_QK, GDN_VW, GDN_VW, GDN_HEADS, GDN_HEADS, NSA_QW, NSA_KVW, NSA_KVW, NSA_KVW, NSA_KVW,
            NSA_KVW, NSA_KVW, 3 * NSA_HEADS)

LANES = 128
VMEM_LIMIT = 56 * 1024 * 1024

F32 = jnp.float32
BF16 = jnp.bfloat16
NT_DIMS = (((1,), (1,)), ((), ()))


def _matmul_kernel(x_ref, w_ref, o_ref):
    o_ref[...] = jnp.dot(x_ref[...], w_ref[...], preferred_element_type=F32)


def matmul(x, w, tm, tn, name):
    m, k = x.shape
    n = w.shape[1]
    assert m % tm == 0 and n % tn == 0
    return pl.pallas_call(
        _matmul_kernel,
        grid=(n // tn, m // tm),
        in_specs=[pl.BlockSpec((tm, k), lambda j, i: (i, 0)), pl.BlockSpec((k, tn), lambda j, i: (0, j))],
        out_specs=pl.BlockSpec((tm, tn), lambda j, i: (i, j)),
        out_shape=jax.ShapeDtypeStruct((m, n), F32),
        compiler_params=pltpu.CompilerParams(dimension_semantics=("parallel", "parallel"),
                                             vmem_limit_bytes=VMEM_LIMIT),
        name=name,
    )(x, w)


def _gelu_tanh(x):
    return 0.5 * x * (1.0 + jnp.tanh(math.sqrt(2.0 / math.pi) * (x + 0.044715 * (x * x * x))))


def _peer_expert_kernel(x_ref, u_ref, v_ref, g_ref, o_ref):
    @pl.when(pl.program_id(1) == 0)
    def _():
        o_ref[...] = jnp.zeros_like(o_ref)

    s = lax.dot_general(x_ref[...], u_ref[...], NT_DIMS, preferred_element_type=F32)
    g = jnp.concatenate([g_ref[:, a, :] for a in range(g_ref.shape[1])], axis=1)
    h = (g * _gelu_tanh(s)).astype(BF16)
    o_ref[...] += jnp.dot(h, v_ref[...], preferred_element_type=F32)


def peer_experts(x, u, v, gate, tn, te):
    n, d = x.shape
    e = u.shape[0]
    assert n % tn == 0 and e % te == 0 and te % PEER_KEYS == 0
    once = pl.Buffered(1)
    return pl.pallas_call(
        _peer_expert_kernel,
        grid=(n // tn, e // te),
        in_specs=[pl.BlockSpec((tn, d), lambda i, j: (i, 0), pipeline_mode=once),
                  pl.BlockSpec((te, d), lambda i, j: (j, 0)), pl.BlockSpec((te, d), lambda i, j: (j, 0)),
                  pl.BlockSpec((tn, te // PEER_KEYS, PEER_KEYS), lambda i, j: (i, j, 0))],
        out_specs=pl.BlockSpec((tn, d), lambda i, j: (i, 0), pipeline_mode=once),
        out_shape=jax.ShapeDtypeStruct((n, d), F32),
        compiler_params=pltpu.CompilerParams(dimension_semantics=("parallel", "arbitrary"),
                                             vmem_limit_bytes=VMEM_LIMIT),
        name="peer_experts",
    )(x, u, v, gate)


def layer_norm(x, g, b):
    mu = jnp.mean(x, axis=-1, keepdims=True)
    var = jnp.mean(jnp.square(x - mu), axis=-1, keepdims=True)
    return (x - mu) * lax.rsqrt(var + LN_EPS) * g + b


def rms_norm(x, g):
    return x * lax.rsqrt(jnp.mean(x * x, axis=-1, keepdims=True) + 1e-6) * g


def l2_normalize(x):
    return x * lax.rsqrt(jnp.sum(x * x, axis=-1, keepdims=True) + 1e-6)


def partial_rope(x, pos):
    half = ROT_DIM // 2
    inv = ROPE_THETA ** (-jnp.arange(half, dtype=F32) / half)
    ang = pos.astype(F32)[:, None] * inv[None, :]
    cos = jnp.cos(ang)[None, :, None, :]
    sin = jnp.sin(ang)[None, :, None, :]
    xr = x[..., :ROT_DIM]
    x1, x2 = xr[..., :half], xr[..., half:]
    rot = jnp.concatenate([x1 * cos - x2 * sin, x2 * cos + x1 * sin], axis=-1)
    return jnp.concatenate([rot, x[..., ROT_DIM:]], axis=-1)


def causal_short_conv(x, buf, w):
    T = x.shape[1]
    xp = jnp.concatenate([buf, x], axis=1)
    y = w[0] * xp[:, 0:T]
    for j in range(1, GDN_CONV):
        y = y + w[j] * xp[:, j:j + T]
    return jax.nn.silu(y), xp[:, xp.shape[1] - (GDN_CONV - 1):]


def gated_delta_rule(q, k, v, g, beta, s0):
    B, T, H, dk = q.shape
    dv = v.shape[-1]
    C = T if T <= GDN_CHUNK else math.gcd(T, GDN_CHUNK)
    n = T // C
    q = l2_normalize(q) * (dk ** -0.5)
    k = l2_normalize(k)

    def chunks(a):
        a = a.reshape((B, n, C, H) + a.shape[3:])
        return jnp.moveaxis(a, (1, 3), (0, 2))

    qc, kc, vc, bc = chunks(q), chunks(k), chunks(v), chunks(beta)
    gc = jnp.cumsum(chunks(g), axis=-1)
    incl = jnp.tril(jnp.ones((C, C), bool))
    strict = jnp.tril(jnp.ones((C, C), bool), -1)
    decay = jnp.exp(jnp.where(incl, gc[..., :, None] - gc[..., None, :], -jnp.inf))
    kb = kc * bc[..., None]
    vb = vc * bc[..., None]
    a = jnp.where(strict, jnp.einsum('nbhid,nbhjd->nbhij', kb, kc) * decay, 0.0)
    eye = jnp.eye(C, dtype=a.dtype)
    t_inv = lax.linalg.triangular_solve(eye + a, jnp.broadcast_to(eye, a.shape), left_side=True, lower=True)
    u = t_inv @ vb
    w = t_inv @ (kb * jnp.exp(gc)[..., None])
    qk = jnp.where(incl, jnp.einsum('nbhid,nbhjd->nbhij', qc, kc) * decay, 0.0)
    q_dec = qc * jnp.exp(gc)[..., None]
    k_dec = kc * jnp.exp(gc[..., -1:] - gc)[..., None]
    g_last = jnp.exp(gc[..., -1])

    def step(s, xs):
        qk_i, u_i, w_i, q_i, k_i, gl = xs
        v_new = u_i - w_i @ s
        o = q_i @ s + qk_i @ v_new
        s = s * gl[..., None, None] + jnp.swapaxes(k_i, -1, -2) @ v_new
        return s, o

    s_final, o = lax.scan(step, s0, (qk, u, w, q_dec, k_dec, g_last))
    o = jnp.moveaxis(o, (0, 2), (1, 3)).reshape(B, T, H, dv)
    return o, s_final


def gdn_mixer(q, k, v, z, a, b, conv_buf, s0, conv_w, a_log, dt_bias, norm_w):
    B, T, _ = q.shape
    qkv, conv_new = causal_short_conv(jnp.concatenate([q, k, v], axis=-1), conv_buf, conv_w)
    q, k, v = jnp.split(qkv, [GDN_QK, 2 * GDN_QK], axis=-1)
    g = -jnp.exp(a_log) * jax.nn.softplus(a + dt_bias)
    beta = jax.nn.sigmoid(b)
    o, s_new = gated_delta_rule(q.reshape(B, T, GDN_HEADS, GDN_DK), k.reshape(B, T, GDN_HEADS, GDN_DK),
                                v.reshape(B, T, GDN_HEADS, GDN_DV), g, beta, s0)
    o = rms_norm(o, norm_w) * jax.nn.silu(z.reshape(B, T, GDN_HEADS, GDN_DV))
    return o.reshape(B, T, GDN_VW), conv_new, s_new


GDN_BLOCK = 128
GDN_HG = 4
SUBLANES = 8


def _sigmoid(x):
    return 1.0 / (1.0 + jnp.exp(-x))


SPLIT_PASSES = 3


def _dot_split(a, b):
    a_hi, b_hi = a.astype(BF16), b.astype(BF16)
    out = jnp.dot(a_hi, b_hi, preferred_element_type=F32)
    if SPLIT_PASSES == 1:
        return out
    a_lo = (a - a_hi.astype(F32)).astype(BF16)
    b_lo = (b - b_hi.astype(F32)).astype(BF16)
    return out + (jnp.dot(a_hi, b_lo, preferred_element_type=F32) + jnp.dot(a_lo, b_hi, preferred_element_type=F32))


def _gdn_prompt_kernel(q_ref, k_ref, v_ref, z_ref, cols_ref, rows_ref, wq_ref, wk_ref, wv_ref, nw_ref,
                       o_ref, sfin_ref):
    C = GDN_BLOCK
    T = q_ref.shape[0]
    grp = pl.program_id(1)
    ri = lax.broadcasted_iota(jnp.int32, (C, C), 0)
    ci = lax.broadcasted_iota(jnp.int32, (C, C), 1)
    incl, strict = ri >= ci, ri > ci
    eye = jnp.where(ri == ci, 1.0, 0.0)
    lane = lax.broadcasted_iota(jnp.int32, (C, LANES), 1)

    def conv_silu(ref, w_ref, i, c, r0):
        sl = slice(i * GDN_DK, (i + 1) * GDN_DK)
        prev = ref[pl.ds(pl.multiple_of(jnp.maximum(r0 - SUBLANES, 0), SUBLANES), SUBLANES), sl]
        xp = jnp.concatenate([jnp.where(c > 0, prev, 0.0), ref[pl.ds(r0, C), sl]], axis=0)
        w = w_ref[:, sl]
        y = w[GDN_CONV - 1:GDN_CONV] * xp[SUBLANES:]
        for j in range(GDN_CONV - 1):
            y = y + w[j:j + 1] * pltpu.roll(xp, GDN_CONV - 1 - j, axis=0)[SUBLANES:]
        return y * _sigmoid(y)

    def chunk(c, states):
        r0 = pl.multiple_of(c * C, C)
        colblk = cols_ref[0, pl.ds(r0, C), :]
        heads = range(GDN_HG)
        hid = [grp * GDN_HG + i for i in heads]
        beta = [jnp.sum(jnp.where(lane == h, colblk, 0.0), axis=1, keepdims=True) for h in hid]
        gc = [jnp.sum(jnp.where(lane == GDN_HEADS + h, colblk, 0.0), axis=1, keepdims=True) for h in hid]
        gc_row = [rows_ref[0, i, pl.ds(c, 1), :] for i in heads]
        g_last = [r[:, C - 1:C] for r in gc_row]
        q = [conv_silu(q_ref, wq_ref, i, c, r0) for i in heads]
        k = [conv_silu(k_ref, wk_ref, i, c, r0) for i in heads]
        v = [conv_silu(v_ref, wv_ref, i, c, r0) for i in heads]
        q = [x * lax.rsqrt(jnp.sum(x * x, axis=1, keepdims=True) + 1e-6) * (GDN_DK ** -0.5) for x in q]
        k = [x * lax.rsqrt(jnp.sum(x * x, axis=1, keepdims=True) + 1e-6) for x in k]
        decay = [jnp.exp(jnp.where(incl, gc[i] - gc_row[i], -jnp.inf)) for i in heads]
        e_gc = [jnp.exp(x) for x in gc]
        kb = [k[i] * beta[i] for i in heads]
        k16 = [x.astype(BF16) for x in k]
        a = [jnp.where(strict, lax.dot_general(kb[i].astype(BF16), k16[i], NT_DIMS, preferred_element_type=F32)
                       * decay[i], 0.0) for i in heads]
        qk = [jnp.where(incl, lax.dot_general(q[i].astype(BF16), k16[i], NT_DIMS, preferred_element_type=F32)
                        * decay[i], 0.0) for i in heads]
        xp = [-x for x in a]
        t_inv = [eye + x for x in xp]
        for _ in range(int(math.log2(C)) - 1):
            xp = [_dot_split(x, x) for x in xp]
            t_inv = [t_inv[i] + _dot_split(t_inv[i], xp[i]) for i in heads]
        t16 = [x.astype(BF16) for x in t_inv]
        u = [jnp.dot(t16[i], (v[i] * beta[i]).astype(BF16), preferred_element_type=F32) for i in heads]
        w = [jnp.dot(t16[i], (kb[i] * e_gc[i]).astype(BF16), preferred_element_type=F32) for i in heads]
        s16 = [x.astype(BF16) for x in states]
        v16 = [(u[i] - jnp.dot(w[i].astype(BF16), s16[i], preferred_element_type=F32)).astype(BF16) for i in heads]
        o = [jnp.dot((q[i] * e_gc[i]).astype(BF16), s16[i], preferred_element_type=F32)
             + jnp.dot(qk[i].astype(BF16), v16[i], preferred_element_type=F32) for i in heads]
        k_dec = [(k[i] * jnp.exp(g_last[i] - gc[i])).astype(BF16) for i in heads]
        new_states = [states[i] * jnp.exp(g_last[i])
                      + lax.dot_general(k_dec[i], v16[i], (((0,), (0,)), ((), ())), preferred_element_type=F32)
                      for i in heads]
        z = z_ref[pl.ds(r0, C), :]
        o = jnp.concatenate([x * lax.rsqrt(jnp.mean(x * x, axis=1, keepdims=True) + 1e-6) * nw_ref[...] for x in o],
                            axis=1)
        o_ref[pl.ds(r0, C), :] = o * (z * _sigmoid(z))
        return tuple(new_states)

    zero = jnp.zeros((GDN_DK, GDN_DV), F32)
    final = lax.fori_loop(0, T // C, chunk, (zero,) * GDN_HG)
    for i in range(GDN_HG):
        sfin_ref[0, i] = final[i]


def gdn_prompt(proj, a, b, conv_w, a_log, dt_bias, norm_w, B, T):
    C, HG, H = GDN_BLOCK, GDN_HG, GDN_HEADS
    assert T % C == 0 and H % HG == 0 and 2 * H <= LANES
    g = -jnp.exp(a_log) * jax.nn.softplus(a + dt_bias)
    beta = jax.nn.sigmoid(b)
    gc = jnp.cumsum(g.reshape(B, T // C, C, H), axis=2)
    cols = jnp.concatenate([beta, gc.reshape(B, T, H), jnp.zeros((B, T, LANES - 2 * H), F32)], axis=-1)
    rows = jnp.moveaxis(gc, 3, 1)
    wblk = HG * GDN_DK
    nq = GDN_QK // wblk
    wide = lambda off: pl.BlockSpec((T, wblk), lambda bi, gi: (bi, off * nq + gi))
    taps = lambda off: pl.BlockSpec((GDN_CONV, wblk), lambda bi, gi: (0, off * nq + gi))
    return pl.pallas_call(
        _gdn_prompt_kernel,
        grid=(B, H // HG),
        in_specs=[wide(0), wide(1), wide(2), wide(3),
                  pl.BlockSpec((1, T, LANES), lambda bi, gi: (bi, 0, 0)),
                  pl.BlockSpec((1, HG, T // C, C), lambda bi, gi: (bi, gi, 0, 0)),
                  taps(0), taps(1), taps(2),
                  pl.BlockSpec((1, GDN_DV), lambda bi, gi: (0, 0))],
        out_specs=[pl.BlockSpec((T, wblk), lambda bi, gi: (bi, gi)),
                   pl.BlockSpec((1, HG, GDN_DK, GDN_DV), lambda bi, gi: (bi, gi, 0, 0))],
        out_shape=[jax.ShapeDtypeStruct((B * T, GDN_VW), F32), jax.ShapeDtypeStruct((B, H, GDN_DK, GDN_DV), F32)],
        compiler_params=pltpu.CompilerParams(dimension_semantics=("parallel", "parallel"),
                                             vmem_limit_bytes=VMEM_LIMIT),
        name="gdn_prompt",
    )(proj, proj, proj, proj, cols, rows, conv_w, conv_w, conv_w, norm_w.reshape(1, GDN_DV))


def compress_blocks(x, pos_emb, w1, w2):
    B, Tk, G, d = x.shape
    nsub = Tk // CMP_STRIDE
    sub = x[:, :nsub * CMP_STRIDE].reshape(B, nsub, CMP_STRIDE, G, d)
    blocks = jnp.concatenate([sub[:, :-1], sub[:, 1:]], axis=2) + pos_emb[:, None, :]
    h = jax.nn.gelu(jnp.einsum('bnjgd,jde->bnge', blocks, w1.reshape(CMP_LEN, d, CMP_HIDDEN)))
    return jnp.einsum('bnge,ef->bngf', h, w2)


def selection_weights(n_cmp, n_slc):
    r = SLC_BLOCK // CMP_STRIDE
    s = CMP_LEN // CMP_STRIDE
    wts = np.convolve(np.ones(r), np.ones(s)).astype(np.float32)
    off = np.arange(n_cmp)[:, None] - r * np.arange(n_slc)[None, :]
    return np.where((off >= 0) & (off < wts.size), wts[np.clip(off, 0, wts.size - 1)], 0.0).astype(np.float32)


PAGE_SIZE = 128
N_PAGES = PAST_LEN // PAGE_SIZE
PAGE_ROWS = PAGE_SIZE * NSA_KV_HEADS
SUB_ROWS = CMP_STRIDE * NSA_KV_HEADS
SUBS_PER_PAGE = PAGE_SIZE // CMP_STRIDE
N_SUB = PAST_LEN // CMP_STRIDE
DEC_SEQ = 8
Q_ROWS = DEC_SEQ * NSA_HEADS
NEW_ROWS = DEC_SEQ * NSA_KV_HEADS


def _sample_compress_kernel(pt_ref, *refs):
    k_pages, v_pages = refs[:N_PAGES], refs[N_PAGES:2 * N_PAGES]
    pos_k, w1_k, w2_k, pos_v, w1_v, w2_v, ok_ref, ov_ref = refs[2 * N_PAGES:]
    G = NSA_KV_HEADS

    def compress(pages, pos_ref, w1_ref, w2_ref, out_ref):
        first = jnp.zeros((G * N_SUB, CMP_HIDDEN), F32)
        second = jnp.zeros((G * N_SUB, CMP_HIDDEN), F32)
        for j in range(CMP_STRIDE):
            x = jnp.concatenate([pg[0, pl.ds(j * G + g, SUBS_PER_PAGE, stride=SUB_ROWS), :]
                                 for g in range(G) for pg in pages], axis=0)
            lo = (x + pos_ref[j:j + 1, :]).astype(BF16)
            hi = (x + pos_ref[CMP_STRIDE + j:CMP_STRIDE + j + 1, :]).astype(BF16)
            first = first + jnp.dot(lo, w1_ref[j * NSA_DIM:(j + 1) * NSA_DIM, :], preferred_element_type=F32)
            second = second + jnp.dot(hi, w1_ref[(CMP_STRIDE + j) * NSA_DIM:(CMP_STRIDE + j + 1) * NSA_DIM, :],
                                      preferred_element_type=F32)
        h = first + pltpu.roll(second, G * N_SUB - 1, axis=0)
        out_ref[0] = jnp.dot(_gelu_tanh(h).astype(BF16), w2_ref[...], preferred_element_type=F32)

    compress(k_pages, pos_k, w1_k, w2_k, ok_ref)
    compress(v_pages, pos_v, w1_v, w2_v, ov_ref)


def _paged_specs(n):
    return [pl.BlockSpec((1, PAGE_ROWS, NSA_DIM), lambda b, pt, p=p: (pt[b, p], 0, 0)) for p in range(N_PAGES)] * n


def sample_compress(page_table, cmp_k, cmp_v, pos_k, w1_k, w2_k, pos_v, w1_v, w2_v):
    db = page_table.shape[0]
    full = lambda shape: pl.BlockSpec(shape, lambda b, pt: (0,) * len(shape))
    out = pl.BlockSpec((1, NSA_KV_HEADS * N_SUB, NSA_DIM), lambda b, pt: (b, 0, 0))
    return pl.pallas_call(
        _sample_compress_kernel,
        grid_spec=pltpu.PrefetchScalarGridSpec(
            num_scalar_prefetch=1, grid=(db,),
            in_specs=_paged_specs(1) + _paged_specs(1)
            + [full((CMP_LEN, NSA_DIM)), full((CMP_LEN * NSA_DIM, CMP_HIDDEN)), full((CMP_HIDDEN, NSA_DIM))] * 2,
            out_specs=[out, out]),
        out_shape=[jax.ShapeDtypeStruct((db, NSA_KV_HEADS * N_SUB, NSA_DIM), F32)] * 2,
        compiler_params=pltpu.CompilerParams(dimension_semantics=("parallel",), vmem_limit_bytes=VMEM_LIMIT),
        name="sample_compress",
    )(page_table, *([cmp_k] * N_PAGES), *([cmp_v] * N_PAGES), pos_k, w1_k.astype(BF16), w2_k.astype(BF16),
      pos_v, w1_v.astype(BF16), w2_v.astype(BF16))


def _sample_attention_kernel(pt_ref, *refs):
    k_pages, v_pages = refs[:N_PAGES], refs[N_PAGES:2 * N_PAGES]
    (wk_ref, wv_ref, kc_ref, vc_ref, q_ref, qr_ref, ksn_ref, vsn_ref, kwn_ref, vwn_ref, m_ref,
     o_ref, s_scr) = refs[2 * N_PAGES:]
    G, R = NSA_KV_HEADS, NSA_GROUP
    scale = NSA_DIM ** -0.5
    n_slc = -(-(PAST_LEN + DEC_SEQ) // SLC_BLOCK)
    blocks_per_page = PAGE_SIZE // SLC_BLOCK

    def iota(shape, axis):
        return lax.broadcasted_iota(jnp.int32, shape, axis)

    def q_side(width):
        r = iota((Q_ROWS, width), 0)
        return r // NSA_HEADS, (r % NSA_HEADS) // R

    def scores(q16, keys):
        return lax.dot_general(q16, keys.astype(BF16), NT_DIMS, preferred_element_type=F32) * scale

    def pad_new(ref):
        return jnp.concatenate([ref[...], jnp.zeros((LANES - NEW_ROWS, NSA_DIM), F32)], axis=0).astype(BF16)

    q16, qr16 = q_ref[...].astype(BF16), qr_ref[...].astype(BF16)

    tq, gq = q_side(G * N_SUB)
    c = iota((Q_ROWS, G * N_SUB), 1)
    ok = (c // N_SUB == gq) & (c % N_SUB < N_SUB - 1)
    s = jnp.where(ok, scores(q16, kc_ref[0]), NEG)
    e = jnp.exp(s - jnp.max(s, axis=1, keepdims=True))
    p = jnp.where(ok, e / jnp.sum(e, axis=1, keepdims=True), 0.0)
    p16 = p.astype(BF16)
    o_ref[0] = jnp.dot(p16, vc_ref[0].astype(BF16), preferred_element_type=F32)

    imp_head = jnp.dot(p16, m_ref[...], preferred_element_type=F32)
    tq, gq = q_side(Q_ROWS)
    cq = iota((Q_ROWS, Q_ROWS), 1)
    same_group = jnp.where((cq // NSA_HEADS == tq) & ((cq % NSA_HEADS) // R == gq), 1.0, 0.0)
    imp = _dot_split(same_group, imp_head)
    tq, gq = q_side(LANES)
    blk = iota((Q_ROWS, LANES), 1)
    pos = PAST_LEN + tq
    cur = pos // SLC_BLOCK
    valid = (blk * SLC_BLOCK <= pos) & (blk < n_slc)
    forced = (blk == 0) | (blk == cur) | (blk == cur - 1)
    score = jnp.where(valid, jnp.where(forced, FORCE, imp), NEG)
    rank = jnp.zeros((Q_ROWS, LANES), F32)
    for i in range(n_slc):
        si = score[:, i:i + 1]
        rank = rank + jnp.where((si > score) | ((si == score) & (i < blk)), 1.0, 0.0)
    sel = jnp.where((rank < SLC_TOPK) & valid, 1.0, 0.0)

    def new_mask():
        tq, gq = q_side(LANES)
        c = iota((Q_ROWS, LANES), 1)
        return (c < NEW_ROWS) & (c % G == gq) & (c // G <= tq)

    tq, gq = q_side(PAGE_ROWS)
    c = iota((Q_ROWS, PAGE_ROWS), 1)
    own_head = c % G == gq
    first_block = c // G < SLC_BLOCK
    for pg in range(N_PAGES):
        picked = jnp.where(first_block, sel[:, blocks_per_page * pg:blocks_per_page * pg + 1],
                           sel[:, blocks_per_page * pg + 1:blocks_per_page * pg + 2])
        s_scr[:, pg * PAGE_ROWS:(pg + 1) * PAGE_ROWS] = jnp.where(own_head & (picked > 0.5),
                                                                   scores(qr16, k_pages[pg][0]), NEG)
    past_cols = N_PAGES * PAGE_ROWS
    new_block = PAST_LEN // SLC_BLOCK
    s_scr[:, past_cols:past_cols + LANES] = jnp.where(new_mask() & (sel[:, new_block:new_block + 1] > 0.5),
                                                      scores(qr16, pad_new(ksn_ref)), NEG)
    s = s_scr[...]
    e = jnp.exp(s - jnp.max(s, axis=1, keepdims=True))
    acc = jnp.dot(e[:, past_cols:].astype(BF16), pad_new(vsn_ref), preferred_element_type=F32)
    for pg in range(N_PAGES):
        acc = acc + jnp.dot(e[:, pg * PAGE_ROWS:(pg + 1) * PAGE_ROWS].astype(BF16), v_pages[pg][0].astype(BF16),
                            preferred_element_type=F32)
    o_ref[1] = acc / jnp.sum(e, axis=1, keepdims=True)

    wrows = WINDOW * G
    tq, gq = q_side(wrows)
    c = iota((Q_ROWS, wrows), 1)
    s_buf = jnp.where((c % G == gq) & (c // G > tq), scores(qr16, wk_ref[0]), NEG)
    s_new = jnp.where(new_mask(), scores(qr16, pad_new(kwn_ref)), NEG)
    m = jnp.maximum(jnp.max(s_buf, axis=1, keepdims=True), jnp.max(s_new, axis=1, keepdims=True))
    e_buf, e_new = jnp.exp(s_buf - m), jnp.exp(s_new - m)
    acc = (jnp.dot(e_buf.astype(BF16), wv_ref[0].astype(BF16), preferred_element_type=F32)
           + jnp.dot(e_new.astype(BF16), pad_new(vwn_ref), preferred_element_type=F32))
    o_ref[2] = acc / (jnp.sum(e_buf, axis=1, keepdims=True) + jnp.sum(e_new, axis=1, keepdims=True))


def sample_attention(page_table, slc_k, slc_v, win_k, win_v, k_cmp, v_cmp, q, q_rot, ks_new, vs_new, kw_new, vw_new):
    db = page_table.shape[0]
    assert WINDOW == win_k.shape[1] // NSA_KV_HEADS and q.shape[0] == db * Q_ROWS
    n_cmp = N_SUB - 1
    n_slc = -(-(PAST_LEN + DEC_SEQ) // SLC_BLOCK)
    sel_m = np.zeros((NSA_KV_HEADS, N_SUB, LANES), np.float32)
    sel_m[:, :n_cmp, :n_slc] = selection_weights(n_cmp, n_slc)[None]
    sel_m = jnp.asarray(sel_m.reshape(NSA_KV_HEADS * N_SUB, LANES), BF16)
    per_seq = lambda rows: pl.BlockSpec((1, rows, NSA_DIM), lambda b, pt: (b, 0, 0))
    flat = lambda rows: pl.BlockSpec((rows, NSA_DIM), lambda b, pt: (b, 0))
    return pl.pallas_call(
        _sample_attention_kernel,
        grid_spec=pltpu.PrefetchScalarGridSpec(
            num_scalar_prefetch=1, grid=(db,),
            in_specs=_paged_specs(1) + _paged_specs(1)
            + [per_seq(WINDOW * NSA_KV_HEADS)] * 2 + [per_seq(NSA_KV_HEADS * N_SUB)] * 2
            + [flat(Q_ROWS)] * 2 + [flat(NEW_ROWS)] * 4
            + [pl.BlockSpec((NSA_KV_HEADS * N_SUB, LANES), lambda b, pt: (0, 0))],
            out_specs=pl.BlockSpec((3, Q_ROWS, NSA_DIM), lambda b, pt: (0, b, 0)),
            scratch_shapes=[pltpu.VMEM((Q_ROWS, N_PAGES * PAGE_ROWS + LANES), F32)]),
        out_shape=jax.ShapeDtypeStruct((3, db * Q_ROWS, NSA_DIM), F32),
        compiler_params=pltpu.CompilerParams(dimension_semantics=("parallel",), vmem_limit_bytes=VMEM_LIMIT),
        name="sample_attention",
    )(page_table, *([slc_k] * N_PAGES), *([slc_v] * N_PAGES), win_k, win_v, k_cmp, v_cmp, q, q_rot,
      ks_new, vs_new, kw_new, vw_new, sel_m)


PROMPT_TQ = 128


def _dot_split_rhs(a16, b):
    b_hi = b.astype(BF16)
    b_lo = (b - b_hi.astype(F32)).astype(BF16)
    return jnp.dot(a16, b_hi, preferred_element_type=F32) + jnp.dot(a16, b_lo, preferred_element_type=F32)


def _prompt_attention_kernel(q_ref, qr_ref, kc_ref, vc_ref, ks_ref, vs_ref, kw_ref, vw_ref, m_ref, x_ref, o_ref):
    R, TQ = NSA_GROUP, PROMPT_TQ
    QR = TQ * R
    T = ks_ref.shape[2]
    n_cmp = T // CMP_STRIDE - 1
    n_slc = -(-T // SLC_BLOCK)
    scale = NSA_DIM ** -0.5
    i = pl.program_id(2)

    def iota(shape, axis):
        return lax.broadcasted_iota(jnp.int32, shape, axis)

    def q_pos(width):
        return i * TQ + iota((QR, width), 0) // R

    def scores(q16, keys):
        return lax.dot_general(q16, keys.astype(BF16), NT_DIMS, preferred_element_type=F32) * scale

    q16, qr16 = q_ref[0, 0].astype(BF16), qr_ref[0, 0].astype(BF16)

    n = iota((QR, LANES), 1)
    pos = q_pos(LANES)
    ok = (n * CMP_STRIDE + (CMP_LEN - 1) <= pos) & (n < n_cmp)
    s = jnp.where(ok, scores(q16, kc_ref[0, 0]), NEG)
    e = jnp.exp(s - jnp.max(s, axis=1, keepdims=True))
    p16 = jnp.where(ok, e / jnp.sum(e, axis=1, keepdims=True), 0.0).astype(BF16)
    o_ref[0, 0, 0] = jnp.dot(p16, vc_ref[0, 0].astype(BF16), preferred_element_type=F32)

    imp_head = jnp.dot(p16, m_ref[...], preferred_element_type=F32)
    rows_of_token = jnp.where(iota((TQ, QR), 1) // R == iota((TQ, QR), 0), 1.0, 0.0).astype(BF16)
    imp = _dot_split_rhs(rows_of_token, imp_head)
    blk = iota((TQ, LANES), 1)
    pos = i * TQ + iota((TQ, LANES), 0)
    cur = pos // SLC_BLOCK
    valid = (blk * SLC_BLOCK <= pos) & (blk < n_slc)
    forced = (blk == 0) | (blk == cur) | (blk == cur - 1)
    score = jnp.where(valid, jnp.where(forced, FORCE, imp), NEG)
    rank = jnp.zeros((TQ, LANES), F32)
    for b in range(n_slc):
        sb = score[:, b:b + 1]
        rank = rank + jnp.where((sb > score) | ((sb == score) & (b < blk)), 1.0, 0.0)
    sel16 = jnp.where((rank < min(SLC_TOPK, n_slc)) & valid, 1.0, 0.0).astype(BF16)

    key_sel_tok = jnp.dot(sel16, x_ref[...], preferred_element_type=F32).astype(BF16)
    token_of_row = jnp.where(iota((QR, TQ), 0) // R == iota((QR, TQ), 1), 1.0, 0.0).astype(BF16)
    key_sel = jnp.dot(token_of_row, key_sel_tok, preferred_element_type=F32)
    ok = (key_sel > 0.5) & (iota((QR, T), 1) <= q_pos(T))
    s = jnp.where(ok, scores(qr16, ks_ref[0, 0]), NEG)
    e = jnp.exp(s - jnp.max(s, axis=1, keepdims=True))
    o_ref[1, 0, 0] = (jnp.dot(e.astype(BF16), vs_ref[0, 0].astype(BF16), preferred_element_type=F32)
                      / jnp.sum(e, axis=1, keepdims=True))

    span = WINDOW + TQ
    start = pl.multiple_of(jnp.maximum(i - WINDOW // TQ, 0) * TQ, TQ)
    k_pos = start + iota((QR, span), 1)
    pos = q_pos(span)
    ok = (k_pos <= pos) & (k_pos > pos - WINDOW)
    s = jnp.where(ok, scores(qr16, kw_ref[0, 0, pl.ds(start, span), :]), NEG)
    e = jnp.exp(s - jnp.max(s, axis=1, keepdims=True))
    o_ref[2, 0, 0] = (jnp.dot(e.astype(BF16), vw_ref[0, 0, pl.ds(start, span), :].astype(BF16),
                              preferred_element_type=F32) / jnp.sum(e, axis=1, keepdims=True))


def prompt_attention(q, q_rot, k_cmp, v_cmp, ks, vs, kw, vw):
    B, T, H, d = q.shape
    G, R, TQ = NSA_KV_HEADS, NSA_GROUP, PROMPT_TQ
    n_cmp = k_cmp.shape[1]
    n_slc = -(-T // SLC_BLOCK)
    assert T % TQ == 0 and n_cmp == T // CMP_STRIDE - 1 and n_cmp <= LANES and n_slc <= LANES and T >= WINDOW + TQ
    by_head = lambda a: jnp.swapaxes(a, 1, 2)
    q_rows = lambda a: jnp.swapaxes(a.reshape(B, T, G, R, d), 1, 2).reshape(B, G, T * R, d)
    pad_cmp = lambda a: jnp.pad(by_head(a), ((0, 0), (0, 0), (0, LANES - n_cmp), (0, 0)))
    sel_m = np.zeros((LANES, LANES), np.float32)
    sel_m[:n_cmp, :n_slc] = selection_weights(n_cmp, n_slc)
    expand = (np.arange(T)[None, :] // SLC_BLOCK == np.arange(LANES)[:, None]).astype(np.float32)
    rows = pl.BlockSpec((1, 1, TQ * R, d), lambda b, g, i: (b, g, i, 0))
    seq = lambda n: pl.BlockSpec((1, 1, n, d), lambda b, g, i: (b, g, 0, 0))
    const = lambda shape: pl.BlockSpec(shape, lambda b, g, i: (0, 0))
    o = pl.pallas_call(
        _prompt_attention_kernel,
        grid=(B, G, T // TQ),
        in_specs=[rows, rows, seq(LANES), seq(LANES), seq(T), seq(T), seq(T), seq(T),
                  const((LANES, LANES)), const((LANES, T))],
        out_specs=pl.BlockSpec((3, 1, 1, TQ * R, d), lambda b, g, i: (0, b, g, i, 0)),
        out_shape=jax.ShapeDtypeStruct((3, B, G, T * R, d), F32),
        compiler_params=pltpu.CompilerParams(dimension_semantics=("parallel", "parallel", "parallel"),
                                             vmem_limit_bytes=VMEM_LIMIT),
        name="prompt_attention",
    )(q_rows(q), q_rows(q_rot), pad_cmp(k_cmp), pad_cmp(v_cmp), by_head(ks), by_head(vs), by_head(kw), by_head(vw),
      jnp.asarray(sel_m, BF16), jnp.asarray(expand, BF16))
    return jnp.swapaxes(o.reshape(3, B, G, T, R, d), 2, 3).reshape(3, B, T, H, d)


ROUTE_TOKENS = LANES


def _top_rows(s, row_id, k, payload=None):
    slot = lax.broadcasted_iota(jnp.int32, (k, s.shape[1]), 0)
    vals = jnp.zeros((k, s.shape[1]), F32)
    picks = jnp.zeros((k, s.shape[1]), F32)
    for t in range(k):
        m = jnp.max(s, axis=0, keepdims=True)
        i = jnp.min(jnp.where(s == m, row_id, jnp.inf), axis=0, keepdims=True)
        hit = row_id == i
        pick = i if payload is None else jnp.sum(jnp.where(hit, payload, 0.0), axis=0, keepdims=True)
        vals = jnp.where(slot == t, m, vals)
        picks = jnp.where(slot == t, pick, picks)
        s = jnp.where(hit, -jnp.inf, s)
    return vals, picks


def _peer_route_kernel(q_ref, keys_ref, gate_ref, e_scr, g_scr):
    T = ROUTE_TOKENS
    half = PEER_QDIM // 2
    key_id = lax.broadcasted_iota(jnp.int32, (PEER_KEYS, T), 0).astype(F32)
    lo = PEER_TOPK // 2
    cand_row = lax.broadcasted_iota(jnp.int32, (lo * PEER_TOPK + lo, T), 0)
    cand_id = jnp.where(cand_row < lo * PEER_TOPK, cand_row,
                        lo * PEER_TOPK + (cand_row - lo * PEER_TOPK) * PEER_TOPK).astype(F32)
    experts, gates = [], []
    for h in range(PEER_HEADS):
        sub = []
        for p in range(2):
            q = q_ref[:, (2 * h + p) * half:(2 * h + p + 1) * half].astype(BF16)
            s = lax.dot_general(keys_ref[2 * h + p], q, NT_DIMS, preferred_element_type=F32)
            sub.append(_top_rows(s, key_id, PEER_TOPK))
        (s1, i1), (s2, i2) = sub
        cand = jnp.concatenate([s1[k:k + 1] + s2 for k in range(lo)] + [s1[lo:] + s2[0:1]], axis=0)
        cand_e = jnp.concatenate([i1[k:k + 1] * float(PEER_KEYS) + i2 for k in range(lo)]
                                 + [i1[lo:] * float(PEER_KEYS) + i2[0:1]], axis=0)
        top_s, top_e = _top_rows(cand, cand_id, PEER_TOPK, payload=cand_e)
        ex = jnp.exp(top_s - top_s[0:1])
        experts.append(top_e)
        gates.append(ex / jnp.sum(ex, axis=0, keepdims=True))
    e_scr[...] = jnp.concatenate(experts, axis=0).T
    g_scr[...] = jnp.concatenate(gates, axis=0).T

    sub_id = lax.broadcasted_iota(jnp.int32, (PEER_KEYS, PEER_HEADS * PEER_TOPK), 0).astype(F32)

    def scatter_token(n, carry):
        e = e_scr[pl.ds(n, 1), :]
        a = jnp.floor(e * (1.0 / PEER_KEYS))
        b = e - a * float(PEER_KEYS)
        lhs = jnp.where(sub_id == a, g_scr[pl.ds(n, 1), :], 0.0)
        lhs_hi = lhs.astype(BF16)
        lhs_lo = (lhs - lhs_hi.astype(F32)).astype(BF16)
        rhs = jnp.where(sub_id == b, 1.0, 0.0).astype(BF16)
        gate_ref[n] = (lax.dot_general(lhs_hi, rhs, NT_DIMS, preferred_element_type=F32)
                       + lax.dot_general(lhs_lo, rhs, NT_DIMS, preferred_element_type=F32))
        return carry

    lax.fori_loop(0, T, scatter_token, 0, unroll=8)


def peer_routing(q, sub_keys):
    n = q.shape[0]
    assert n % ROUTE_TOKENS == 0
    keys = sub_keys.reshape(PEER_HEADS * 2, PEER_KEYS, PEER_QDIM // 2).astype(BF16)
    j = PEER_HEADS * PEER_TOPK
    return pl.pallas_call(
        _peer_route_kernel,
        grid=(n // ROUTE_TOKENS,),
        in_specs=[pl.BlockSpec((ROUTE_TOKENS, q.shape[1]), lambda i: (i, 0)),
                  pl.BlockSpec(keys.shape, lambda i: (0, 0, 0))],
        out_specs=pl.BlockSpec((ROUTE_TOKENS, PEER_KEYS, PEER_KEYS), lambda i: (i, 0, 0)),
        out_shape=jax.ShapeDtypeStruct((n, PEER_KEYS, PEER_KEYS), F32),
        scratch_shapes=[pltpu.VMEM((ROUTE_TOKENS, j), F32), pltpu.VMEM((ROUTE_TOKENS, j), F32)],
        compiler_params=pltpu.CompilerParams(dimension_semantics=("parallel",), vmem_limit_bytes=VMEM_LIMIT),
        name="peer_routing",
    )(q, keys)


def _permuted_in_weight(w_in):
    cuts = np.cumsum(IN_SIZES)[:-1].tolist()
    parts = jnp.split(w_in, cuts, axis=-1)
    wide = [parts[i] for i in (0, 1, 2, 3, 6, 7, 8, 9, 10, 11, 12)]
    narrow = [parts[i] for i in (4, 5, 13)]
    n_narrow = sum(p.shape[1] for p in narrow)
    pad = jnp.zeros((w_in.shape[0], LANES - n_narrow), w_in.dtype)
    return jnp.concatenate(wide, axis=-1).astype(BF16), jnp.concatenate(narrow + [pad], axis=-1).astype(BF16)


def _split_projection(wide, narrow, B, T):
    sizes = [IN_SIZES[i] for i in (0, 1, 2, 3, 6, 7, 8, 9, 10, 11, 12)]
    cuts = np.cumsum(sizes)[:-1].tolist()
    gq, gk, gv, gz, nq, kc, vc, ks, vs, kw, vw = [p.reshape(B, T, -1) for p in jnp.split(wide, cuts, axis=-1)]
    ga = narrow[:, :GDN_HEADS].reshape(B, T, -1)
    gb = narrow[:, GDN_HEADS:2 * GDN_HEADS].reshape(B, T, -1)
    ngate = narrow[:, 2 * GDN_HEADS:2 * GDN_HEADS + 3 * NSA_HEADS].reshape(B, T, -1)
    return gq, gk, gv, gz, ga, gb, nq, kc, vc, ks, vs, kw, vw, ngate


def _prompt_group(proj, B, T, pos, gdn, weights):
    (conv_w, a_log, dt_bias, norm_w, pos_k, w1_k, w2_k, pos_v, w1_v, w2_v) = weights
    H, G, hd = NSA_HEADS, NSA_KV_HEADS, NSA_DIM
    (gq, gk, gv, gz, ga, gb, nq, kc, vc, ks, vs, kw, vw, ngate) = proj
    o_gdn, conv_new, ssm_new = gdn(gq, gk, gv, gz, ga, gb)
    q = nq.reshape(B, T, H, hd)
    q_rot = partial_rope(q, pos)
    kc = kc.reshape(B, T, G, hd)
    vc = vc.reshape(B, T, G, hd)
    ks = partial_rope(ks.reshape(B, T, G, hd), pos)
    vs = vs.reshape(B, T, G, hd)
    kw = partial_rope(kw.reshape(B, T, G, hd), pos)
    vw = vw.reshape(B, T, G, hd)
    k_cmp = compress_blocks(kc, pos_k, w1_k, w2_k)
    v_cmp = compress_blocks(vc, pos_v, w1_v, w2_v)
    o3 = prompt_attention(q, q_rot, k_cmp, v_cmp, ks, vs, kw, vw)
    keep = min(WINDOW, T)
    win_k_new, win_v_new = kw[:, T - keep:], vw[:, T - keep:]
    gate = jax.nn.sigmoid(ngate).reshape(B, T, 3, H, 1)
    o_nsa = gate[:, :, 0] * o3[0] + gate[:, :, 1] * o3[1] + gate[:, :, 2] * o3[2]
    mix_in = jnp.concatenate([o_gdn, o_nsa.reshape(B, T, NSA_QW)], axis=-1)
    return mix_in.reshape(B * T, D_MODEL), (kc, vc, ks, vs, win_k_new, win_v_new, conv_new, ssm_new)


def _sample_group(proj, DB, L, pos, page_table, pools, win_buf, gdn_state, weights):
    (conv_w, a_log, dt_bias, norm_w, pos_k, w1_k, w2_k, pos_v, w1_v, w2_v) = weights
    H, G, hd = NSA_HEADS, NSA_KV_HEADS, NSA_DIM
    assert L == DEC_SEQ and page_table.shape[1] == N_PAGES and pools[0].shape[1] == PAGE_SIZE
    (gq, gk, gv, gz, ga, gb, nq, kc, vc, ks, vs, kw, vw, ngate) = proj
    conv_buf, ssm0 = gdn_state
    o_gdn, conv_new, ssm_new = gdn_mixer(gq, gk, gv, gz, ga, gb, conv_buf, ssm0, conv_w, a_log, dt_bias, norm_w)
    q = nq.reshape(DB, L, H, hd)
    q_rot = partial_rope(q, pos)
    kc = kc.reshape(DB, L, G, hd)
    vc = vc.reshape(DB, L, G, hd)
    ks = partial_rope(ks.reshape(DB, L, G, hd), pos)
    vs = vs.reshape(DB, L, G, hd)
    kw = partial_rope(kw.reshape(DB, L, G, hd), pos)
    vw = vw.reshape(DB, L, G, hd)
    rows = lambda pool: pool.reshape(pool.shape[0], PAGE_ROWS, hd)
    cmp_k, cmp_v, slc_k, slc_v = [rows(p) for p in pools]
    k_buf, v_buf = win_buf
    k_cmp, v_cmp = sample_compress(page_table, cmp_k, cmp_v, pos_k, w1_k, w2_k, pos_v, w1_v, w2_v)
    new_rows = lambda a: a.reshape(DB * L * G, hd)
    o3 = sample_attention(page_table, slc_k, slc_v, k_buf.reshape(DB, WINDOW * G, hd), v_buf.reshape(DB, WINDOW * G, hd),
                          k_cmp, v_cmp, q.reshape(DB * L * H, hd), q_rot.reshape(DB * L * H, hd),
                          new_rows(ks), new_rows(vs), new_rows(kw), new_rows(vw)).reshape(3, DB, L, H, hd)
    gate = jax.nn.sigmoid(ngate).reshape(DB, L, 3, H, 1)
    o_nsa = gate[:, :, 0] * o3[0] + gate[:, :, 1] * o3[1] + gate[:, :, 2] * o3[2]
    win_k_new = jnp.concatenate([k_buf, kw], axis=1)[:, L:]
    win_v_new = jnp.concatenate([v_buf, vw], axis=1)[:, L:]
    mix_in = jnp.concatenate([o_gdn, o_nsa.reshape(DB, L, NSA_QW)], axis=-1)
    return mix_in.reshape(DB * L, D_MODEL), (kc, vc, ks, vs, win_k_new, win_v_new, conv_new, ssm_new)


def kernel(x_prompt, x_sample, cache_cmp_k, cache_cmp_v, cache_slc_k, cache_slc_v, cache_win_k, cache_win_v,
           state_conv, state_ssm, page_table, w_in, gdn_conv_w, gdn_a_log, gdn_dt_bias, gdn_norm_w,
           cmp_pos_k, cmp_w1_k, cmp_w2_k, cmp_pos_v, cmp_w1_v, cmp_w2_v, w_out, ln1_g, ln1_b,
           peer_w_query, peer_sub_keys, peer_u, peer_v, ln2_g, ln2_b):
    assert w_in.shape[0] == DEPTH == 1
    G, hd = NSA_KV_HEADS, NSA_DIM
    B, T, _ = x_prompt.shape
    DB, L, _ = x_sample.shape
    NP, NS = B * T, DB * L
    pos_p = jnp.arange(T, dtype=jnp.int32)
    pos_s = PAST_LEN + jnp.arange(L, dtype=jnp.int32)
    l = 0
    x_all = jnp.concatenate([x_prompt.reshape(NP, D_MODEL), x_sample.reshape(NS, D_MODEL)], axis=0)

    w_wide, w_narrow = _permuted_in_weight(w_in[l])
    x_bf = x_all.astype(BF16)
    proj_wide = matmul(x_bf, w_wide, 512, 1024, "in_proj_wide")
    proj_narrow = matmul(x_bf, w_narrow, 512, LANES, "in_proj_narrow")

    weights = (gdn_conv_w[l], gdn_a_log[l], gdn_dt_bias[l], gdn_norm_w[l], cmp_pos_k[l], cmp_w1_k[l], cmp_w2_k[l],
               cmp_pos_v[l], cmp_w1_v[l], cmp_w2_v[l])
    def gdn_prompt_group(gq, gk, gv, gz, ga, gb):
        o, ssm = gdn_prompt(proj_wide, ga, gb, gdn_conv_w[l], gdn_a_log[l], gdn_dt_bias[l], gdn_norm_w[l], B, T)
        conv_new = jnp.concatenate([gq, gk, gv], axis=-1)[:, T - (GDN_CONV - 1):]
        return o.reshape(B, T, GDN_VW), conv_new, ssm

    mix_p, new_p = _prompt_group(_split_projection(proj_wide[:NP], proj_narrow[:NP], B, T), B, T, pos_p,
                                 gdn_prompt_group, weights)

    mix_s, new_s = _sample_group(_split_projection(proj_wide[NP:], proj_narrow[NP:], DB, L), DB, L, pos_s, page_table,
                                 (cache_cmp_k[l], cache_cmp_v[l], cache_slc_k[l], cache_slc_v[l]),
                                 (cache_win_k[l], cache_win_v[l]), (state_conv[l], state_ssm[l]), weights)

    mix_in = jnp.concatenate([mix_p, mix_s], axis=0).astype(BF16)
    mixed = matmul(mix_in, w_out[l].astype(BF16), 512, 1024, "out_proj")
    x1 = layer_norm(DN_ALPHA * x_all + mixed, ln1_g[l], ln1_b[l])

    x1_bf = x1.astype(BF16)
    pq = matmul(x1_bf, peer_w_query[l].astype(BF16), 512, 1024, "peer_query")
    gate = peer_routing(pq, peer_sub_keys[l])
    y = peer_experts(x1_bf, peer_u[l].astype(BF16), peer_v[l].astype(BF16), gate, 512, 1024)
    x2 = layer_norm(DN_ALPHA * x1 + y, ln2_g[l], ln2_b[l])

    outs_p = [a[None] for a in new_p]
    outs_s = [a[None] for a in new_s]
    return (x2[:NP].reshape(B, T, D_MODEL), x2[NP:].reshape(DB, L, D_MODEL), *outs_p, *outs_s)
```

```python
import functools
import math

import jax
import jax.numpy as jnp
import numpy as np
from jax import lax
from jax.experimental import pallas as pl
from jax.experimental.pallas import tpu as pltpu

D_MODEL = 4096
PAST_LEN = 2048
GDN_DK = 128
GDN_DV = 128
GDN_HEADS = D_MODEL // 2 // GDN_DV
GDN_CONV = 4
GDN_CHUNK = 64
GDN_QK = GDN_HEADS * GDN_DK
GDN_VW = GDN_HEADS * GDN_DV
NSA_DIM = 128
NSA_HEADS = (D_MODEL - GDN_VW) // NSA_DIM
NSA_GROUP = 4
NSA_KV_HEADS = NSA_HEADS // NSA_GROUP
NSA_QW = NSA_HEADS * NSA_DIM
NSA_KVW = NSA_KV_HEADS * NSA_DIM
CMP_STRIDE = 16
CMP_LEN = 2 * CMP_STRIDE
CMP_HIDDEN = NSA_DIM
SLC_BLOCK = 64
SLC_TOPK = 16
WINDOW = 512
ROT_DIM = NSA_DIM // 4
ROPE_THETA = 500000.0
PEER_HEADS = 8
PEER_KEYS = 128
PEER_EXPERTS = PEER_KEYS * PEER_KEYS
PEER_TOPK = 16
PEER_QDIM = 256
DEPTH = 1
DN_ALPHA = (2.0 * DEPTH) ** 0.25
LN_EPS = 1e-5
NEG = -1e30
FORCE = 1e30
IN_SIZES = (GDN_QK, GDN_QK, GDN_VW, GDN_VW, GDN_HEADS, GDN_HEADS, NSA_QW, NSA_KVW, NSA_KVW, NSA_KVW, NSA_KVW,
            NSA_KVW, NSA_KVW, 3 * NSA_HEADS)

LANES = 128
VMEM_LIMIT = 56 * 1024 * 1024

F32 = jnp.float32
BF16 = jnp.bfloat16
NT_DIMS = (((1,), (1,)), ((), ()))


def _matmul_kernel(x_ref, w_ref, o_ref):
    o_ref[...] = jnp.dot(x_ref[...], w_ref[...], preferred_element_type=F32)


def matmul(x, w, tm, tn, name):
    m, k = x.shape
    n = w.shape[1]
    assert m % tm == 0 and n % tn == 0
    return pl.pallas_call(
        _matmul_kernel,
        grid=(n // tn, m // tm),
        in_specs=[pl.BlockSpec((tm, k), lambda j, i: (i, 0)), pl.BlockSpec((k, tn), lambda j, i: (0, j))],
        out_specs=pl.BlockSpec((tm, tn), lambda j, i: (i, j)),
        out_shape=jax.ShapeDtypeStruct((m, n), F32),
        compiler_params=pltpu.CompilerParams(dimension_semantics=("parallel", "parallel"),
                                             vmem_limit_bytes=VMEM_LIMIT),
        name=name,
    )(x, w)


def _matmul_residual_ln_kernel(x_ref, w_ref, r_ref, g_ref, b_ref, o_ref, o16_ref, acc_ref):
    k = pl.program_id(1)

    @pl.when(k == 0)
    def _():
        acc_ref[...] = jnp.zeros_like(acc_ref)

    acc_ref[...] += jnp.dot(x_ref[...], w_ref[...], preferred_element_type=F32)

    @pl.when(k == pl.num_programs(1) - 1)
    def _():
        y = DN_ALPHA * r_ref[...] + acc_ref[...]
        mu = jnp.mean(y, axis=-1, keepdims=True)
        var = jnp.mean(jnp.square(y - mu), axis=-1, keepdims=True)
        out = (y - mu) * lax.rsqrt(var + LN_EPS) * g_ref[...] + b_ref[...]
        o_ref[...] = out
        o16_ref[...] = out.astype(BF16)


def matmul_residual_ln(x, w, res, g, b, tm, tk, name):
    m, kdim = x.shape
    n = w.shape[1]
    assert m % tm == 0 and kdim % tk == 0
    row = lambda i, k: (i, 0)
    return pl.pallas_call(
        _matmul_residual_ln_kernel,
        grid=(m // tm, kdim // tk),
        in_specs=[pl.BlockSpec((tm, tk), lambda i, k: (i, k)), pl.BlockSpec((tk, n), lambda i, k: (k, 0)),
                  pl.BlockSpec((tm, n), row), pl.BlockSpec((1, n), lambda i, k: (0, 0)),
                  pl.BlockSpec((1, n), lambda i, k: (0, 0))],
        out_specs=[pl.BlockSpec((tm, n), row), pl.BlockSpec((tm, n), row)],
        out_shape=[jax.ShapeDtypeStruct((m, n), F32), jax.ShapeDtypeStruct((m, n), BF16)],
        scratch_shapes=[pltpu.VMEM((tm, n), F32)],
        compiler_params=pltpu.CompilerParams(dimension_semantics=("parallel", "arbitrary"),
                                             vmem_limit_bytes=VMEM_LIMIT),
        name=name,
    )(x, w, res, g.reshape(1, n), b.reshape(1, n))


def _gelu_tanh(x):
    return 0.5 * x * (1.0 + jnp.tanh(math.sqrt(2.0 / math.pi) * (x + 0.044715 * (x * x * x))))


def _peer_expert_kernel(x_ref, u_ref, v_ref, g_ref, o_ref):
    @pl.when(pl.program_id(1) == 0)
    def _():
        o_ref[...] = jnp.zeros_like(o_ref)

    s = lax.dot_general(x_ref[...], u_ref[...], NT_DIMS, preferred_element_type=F32)
    g = jnp.concatenate([g_ref[:, a, :] for a in range(g_ref.shape[1])], axis=1)
    h = (g * _gelu_tanh(s)).astype(BF16)
    o_ref[...] += jnp.dot(h, v_ref[...], preferred_element_type=F32)


def peer_experts(x, u, v, gate, tn, te):
    n, d = x.shape
    e = u.shape[0]
    assert n % tn == 0 and e % te == 0 and te % PEER_KEYS == 0
    once = pl.Buffered(1)
    return pl.pallas_call(
        _peer_expert_kernel,
        grid=(n // tn, e // te),
        in_specs=[pl.BlockSpec((tn, d), lambda i, j: (i, 0), pipeline_mode=once),
                  pl.BlockSpec((te, d), lambda i, j: (j, 0)), pl.BlockSpec((te, d), lambda i, j: (j, 0)),
                  pl.BlockSpec((tn, te // PEER_KEYS, PEER_KEYS), lambda i, j: (i, j, 0))],
        out_specs=pl.BlockSpec((tn, d), lambda i, j: (i, 0), pipeline_mode=once),
        out_shape=jax.ShapeDtypeStruct((n, d), F32),
        compiler_params=pltpu.CompilerParams(dimension_semantics=("parallel", "arbitrary"),
                                             vmem_limit_bytes=VMEM_LIMIT),
        name="peer_experts",
    )(x, u, v, gate)


def layer_norm(x, g, b):
    mu = jnp.mean(x, axis=-1, keepdims=True)
    var = jnp.mean(jnp.square(x - mu), axis=-1, keepdims=True)
    return (x - mu) * lax.rsqrt(var + LN_EPS) * g + b


def rms_norm(x, g):
    return x * lax.rsqrt(jnp.mean(x * x, axis=-1, keepdims=True) + 1e-6) * g


def l2_normalize(x):
    return x * lax.rsqrt(jnp.sum(x * x, axis=-1, keepdims=True) + 1e-6)


def partial_rope(x, pos):
    half = ROT_DIM // 2
    inv = ROPE_THETA ** (-jnp.arange(half, dtype=F32) / half)
    ang = pos.astype(F32)[:, None] * inv[None, :]
    cos = jnp.cos(ang)[None, :, None, :]
    sin = jnp.sin(ang)[None, :, None, :]
    xr = x[..., :ROT_DIM]
    x1, x2 = xr[..., :half], xr[..., half:]
    rot = jnp.concatenate([x1 * cos - x2 * sin, x2 * cos + x1 * sin], axis=-1)
    return jnp.concatenate([rot, x[..., ROT_DIM:]], axis=-1)


def causal_short_conv(x, buf, w):
    T = x.shape[1]
    xp = jnp.concatenate([buf, x], axis=1)
    y = w[0] * xp[:, 0:T]
    for j in range(1, GDN_CONV):
        y = y + w[j] * xp[:, j:j + T]
    return jax.nn.silu(y), xp[:, xp.shape[1] - (GDN_CONV - 1):]


def gated_delta_rule(q, k, v, g, beta, s0):
    B, T, H, dk = q.shape
    dv = v.shape[-1]
    C = T if T <= GDN_CHUNK else math.gcd(T, GDN_CHUNK)
    n = T // C
    q = l2_normalize(q) * (dk ** -0.5)
    k = l2_normalize(k)

    def chunks(a):
        a = a.reshape((B, n, C, H) + a.shape[3:])
        return jnp.moveaxis(a, (1, 3), (0, 2))

    qc, kc, vc, bc = chunks(q), chunks(k), chunks(v), chunks(beta)
    gc = jnp.cumsum(chunks(g), axis=-1)
    incl = jnp.tril(jnp.ones((C, C), bool))
    strict = jnp.tril(jnp.ones((C, C), bool), -1)
    decay = jnp.exp(jnp.where(incl, gc[..., :, None] - gc[..., None, :], -jnp.inf))
    kb = kc * bc[..., None]
    vb = vc * bc[..., None]
    a = jnp.where(strict, jnp.einsum('nbhid,nbhjd->nbhij', kb, kc) * decay, 0.0)
    eye = jnp.eye(C, dtype=a.dtype)
    t_inv = lax.linalg.triangular_solve(eye + a, jnp.broadcast_to(eye, a.shape), left_side=True, lower=True)
    u = t_inv @ vb
    w = t_inv @ (kb * jnp.exp(gc)[..., None])
    qk = jnp.where(incl, jnp.einsum('nbhid,nbhjd->nbhij', qc, kc) * decay, 0.0)
    q_dec = qc * jnp.exp(gc)[..., None]
    k_dec = kc * jnp.exp(gc[..., -1:] - gc)[..., None]
    g_last = jnp.exp(gc[..., -1])

    def step(s, xs):
        qk_i, u_i, w_i, q_i, k_i, gl = xs
        v_new = u_i - w_i @ s
        o = q_i @ s + qk_i @ v_new
        s = s * gl[..., None, None] + jnp.swapaxes(k_i, -1, -2) @ v_new
        return s, o

    s_final, o = lax.scan(step, s0, (qk, u, w, q_dec, k_dec, g_last))
    o = jnp.moveaxis(o, (0, 2), (1, 3)).reshape(B, T, H, dv)
    return o, s_final


def gdn_mixer(q, k, v, z, a, b, conv_buf, s0, conv_w, a_log, dt_bias, norm_w):
    B, T, _ = q.shape
    qkv, conv_new = causal_short_conv(jnp.concatenate([q, k, v], axis=-1), conv_buf, conv_w)
    q, k, v = jnp.split(qkv, [GDN_QK, 2 * GDN_QK], axis=-1)
    g = -jnp.exp(a_log) * jax.nn.softplus(a + dt_bias)
    beta = jax.nn.sigmoid(b)
    o, s_new = gated_delta_rule(q.reshape(B, T, GDN_HEADS, GDN_DK), k.reshape(B, T, GDN_HEADS, GDN_DK),
                                v.reshape(B, T, GDN_HEADS, GDN_DV), g, beta, s0)
    o = rms_norm(o, norm_w) * jax.nn.silu(z.reshape(B, T, GDN_HEADS, GDN_DV))
    return o.reshape(B, T, GDN_VW), conv_new, s_new


GDN_BLOCK = 128
GDN_HG = 4
SUBLANES = 8


def _sigmoid(x):
    return 1.0 / (1.0 + jnp.exp(-x))


SPLIT_PASSES = 3


def _dot_split(a, b):
    a_hi, b_hi = a.astype(BF16), b.astype(BF16)
    out = jnp.dot(a_hi, b_hi, preferred_element_type=F32)
    if SPLIT_PASSES == 1:
        return out
    a_lo = (a - a_hi.astype(F32)).astype(BF16)
    b_lo = (b - b_hi.astype(F32)).astype(BF16)
    return out + (jnp.dot(a_hi, b_lo, preferred_element_type=F32) + jnp.dot(a_lo, b_hi, preferred_element_type=F32))


def _gdn_prompt_kernel(q_ref, k_ref, v_ref, z_ref, cols_ref, rows_ref, wq_ref, wk_ref, wv_ref, nw_ref,
                       o_ref, sfin_ref):
    C = GDN_BLOCK
    T = q_ref.shape[0]
    grp = pl.program_id(1)
    ri = lax.broadcasted_iota(jnp.int32, (C, C), 0)
    ci = lax.broadcasted_iota(jnp.int32, (C, C), 1)
    incl, strict = ri >= ci, ri > ci
    eye = jnp.where(ri == ci, 1.0, 0.0)
    lane = lax.broadcasted_iota(jnp.int32, (C, LANES), 1)

    def conv_silu(ref, w_ref, i, c, r0):
        sl = slice(i * GDN_DK, (i + 1) * GDN_DK)
        prev = ref[pl.ds(pl.multiple_of(jnp.maximum(r0 - SUBLANES, 0), SUBLANES), SUBLANES), sl]
        xp = jnp.concatenate([jnp.where(c > 0, prev, 0.0), ref[pl.ds(r0, C), sl]], axis=0)
        w = w_ref[:, sl]
        y = w[GDN_CONV - 1:GDN_CONV] * xp[SUBLANES:]
        for j in range(GDN_CONV - 1):
            y = y + w[j:j + 1] * pltpu.roll(xp, GDN_CONV - 1 - j, axis=0)[SUBLANES:]
        return y * _sigmoid(y)

    def chunk(c, states):
        r0 = pl.multiple_of(c * C, C)
        colblk = cols_ref[0, pl.ds(r0, C), :]
        heads = range(GDN_HG)
        hid = [grp * GDN_HG + i for i in heads]
        beta = [jnp.sum(jnp.where(lane == h, colblk, 0.0), axis=1, keepdims=True) for h in hid]
        gc = [jnp.sum(jnp.where(lane == GDN_HEADS + h, colblk, 0.0), axis=1, keepdims=True) for h in hid]
        gc_row = [rows_ref[0, i, pl.ds(c, 1), :] for i in heads]
        g_last = [r[:, C - 1:C] for r in gc_row]
        q = [conv_silu(q_ref, wq_ref, i, c, r0) for i in heads]
        k = [conv_silu(k_ref, wk_ref, i, c, r0) for i in heads]
        v = [conv_silu(v_ref, wv_ref, i, c, r0) for i in heads]
        q = [x * lax.rsqrt(jnp.sum(x * x, axis=1, keepdims=True) + 1e-6) * (GDN_DK ** -0.5) for x in q]
        k = [x * lax.rsqrt(jnp.sum(x * x, axis=1, keepdims=True) + 1e-6) for x in k]
        decay = [jnp.exp(jnp.where(incl, gc[i] - gc_row[i], -jnp.inf)) for i in heads]
        e_gc = [jnp.exp(x) for x in gc]
        kb = [k[i] * beta[i] for i in heads]
        k16 = [x.astype(BF16) for x in k]
        a = [jnp.where(strict, lax.dot_general(kb[i].astype(BF16), k16[i], NT_DIMS, preferred_element_type=F32)
                       * decay[i], 0.0) for i in heads]
        qk = [jnp.where(incl, lax.dot_general(q[i].astype(BF16), k16[i], NT_DIMS, preferred_element_type=F32)
                        * decay[i], 0.0) for i in heads]
        xp = [-x for x in a]
        t_inv = [eye + x for x in xp]
        for _ in range(int(math.log2(C)) - 1):
            xp = [_dot_split(x, x) for x in xp]
            t_inv = [t_inv[i] + _dot_split(t_inv[i], xp[i]) for i in heads]
        t16 = [x.astype(BF16) for x in t_inv]
        u = [jnp.dot(t16[i], (v[i] * beta[i]).astype(BF16), preferred_element_type=F32) for i in heads]
        w = [jnp.dot(t16[i], (kb[i] * e_gc[i]).astype(BF16), preferred_element_type=F32) for i in heads]
        s16 = [x.astype(BF16) for x in states]
        v16 = [(u[i] - jnp.dot(w[i].astype(BF16), s16[i], preferred_element_type=F32)).astype(BF16) for i in heads]
        o = [jnp.dot((q[i] * e_gc[i]).astype(BF16), s16[i], preferred_element_type=F32)
             + jnp.dot(qk[i].astype(BF16), v16[i], preferred_element_type=F32) for i in heads]
        k_dec = [(k[i] * jnp.exp(g_last[i] - gc[i])).astype(BF16) for i in heads]
        new_states = [states[i] * jnp.exp(g_last[i])
                      + lax.dot_general(k_dec[i], v16[i], (((0,), (0,)), ((), ())), preferred_element_type=F32)
                      for i in heads]
        z = z_ref[pl.ds(r0, C), :]
        o = jnp.concatenate([x * lax.rsqrt(jnp.mean(x * x, axis=1, keepdims=True) + 1e-6) * nw_ref[...] for x in o],
                            axis=1)
        o_ref[pl.ds(r0, C), :] = o * (z * _sigmoid(z))
        return tuple(new_states)

    zero = jnp.zeros((GDN_DK, GDN_DV), F32)
    final = lax.fori_loop(0, T // C, chunk, (zero,) * GDN_HG)
    for i in range(GDN_HG):
        sfin_ref[0, i] = final[i]


def gdn_prompt(proj, a, b, conv_w, a_log, dt_bias, norm_w, B, T):
    C, HG, H = GDN_BLOCK, GDN_HG, GDN_HEADS
    assert T % C == 0 and H % HG == 0 and 2 * H <= LANES
    g = -jnp.exp(a_log) * jax.nn.softplus(a + dt_bias)
    beta = jax.nn.sigmoid(b)
    gc = jnp.cumsum(g.reshape(B, T // C, C, H), axis=2)
    cols = jnp.concatenate([beta, gc.reshape(B, T, H), jnp.zeros((B, T, LANES - 2 * H), F32)], axis=-1)
    rows = jnp.moveaxis(gc, 3, 1)
    wblk = HG * GDN_DK
    nq = GDN_QK // wblk
    wide = lambda off: pl.BlockSpec((T, wblk), lambda bi, gi: (bi, off * nq + gi))
    taps = lambda off: pl.BlockSpec((GDN_CONV, wblk), lambda bi, gi: (0, off * nq + gi))
    return pl.pallas_call(
        _gdn_prompt_kernel,
        grid=(B, H // HG),
        in_specs=[wide(0), wide(1), wide(2), wide(3),
                  pl.BlockSpec((1, T, LANES), lambda bi, gi: (bi, 0, 0)),
                  pl.BlockSpec((1, HG, T // C, C), lambda bi, gi: (bi, gi, 0, 0)),
                  taps(0), taps(1), taps(2),
                  pl.BlockSpec((1, GDN_DV), lambda bi, gi: (0, 0))],
        out_specs=[pl.BlockSpec((T, wblk), lambda bi, gi: (bi, gi)),
                   pl.BlockSpec((1, HG, GDN_DK, GDN_DV), lambda bi, gi: (bi, gi, 0, 0))],
        out_shape=[jax.ShapeDtypeStruct((B * T, GDN_VW), F32), jax.ShapeDtypeStruct((B, H, GDN_DK, GDN_DV), F32)],
        compiler_params=pltpu.CompilerParams(dimension_semantics=("parallel", "parallel"),
                                             vmem_limit_bytes=VMEM_LIMIT),
        name="gdn_prompt",
    )(proj, proj, proj, proj, cols, rows, conv_w, conv_w, conv_w, norm_w.reshape(1, GDN_DV))


def compress_blocks(x, pos_emb, w1, w2):
    B, Tk, G, d = x.shape
    nsub = Tk // CMP_STRIDE
    sub = x[:, :nsub * CMP_STRIDE].reshape(B, nsub, CMP_STRIDE, G, d)
    blocks = jnp.concatenate([sub[:, :-1], sub[:, 1:]], axis=2) + pos_emb[:, None, :]
    h = jax.nn.gelu(jnp.einsum('bnjgd,jde->bnge', blocks, w1.reshape(CMP_LEN, d, CMP_HIDDEN)))
    return jnp.einsum('bnge,ef->bngf', h, w2)


def selection_weights(n_cmp, n_slc):
    r = SLC_BLOCK // CMP_STRIDE
    s = CMP_LEN // CMP_STRIDE
    wts = np.convolve(np.ones(r), np.ones(s)).astype(np.float32)
    off = np.arange(n_cmp)[:, None] - r * np.arange(n_slc)[None, :]
    return np.where((off >= 0) & (off < wts.size), wts[np.clip(off, 0, wts.size - 1)], 0.0).astype(np.float32)


PAGE_SIZE = 128
N_PAGES = PAST_LEN // PAGE_SIZE
PAGE_ROWS = PAGE_SIZE * NSA_KV_HEADS
SUB_ROWS = CMP_STRIDE * NSA_KV_HEADS
SUBS_PER_PAGE = PAGE_SIZE // CMP_STRIDE
N_SUB = PAST_LEN // CMP_STRIDE
DEC_SEQ = 8
Q_ROWS = DEC_SEQ * NSA_HEADS
NEW_ROWS = DEC_SEQ * NSA_KV_HEADS


def _sample_compress_kernel(pt_ref, *refs):
    k_pages, v_pages = refs[:N_PAGES], refs[N_PAGES:2 * N_PAGES]
    pos_k, w1_k, w2_k, pos_v, w1_v, w2_v, ok_ref, ov_ref = refs[2 * N_PAGES:]
    G = NSA_KV_HEADS

    def compress(pages, pos_ref, w1_ref, w2_ref, out_ref):
        first = jnp.zeros((G * N_SUB, CMP_HIDDEN), F32)
        second = jnp.zeros((G * N_SUB, CMP_HIDDEN), F32)
        for j in range(CMP_STRIDE):
            x = jnp.concatenate([pg[0, pl.ds(j * G + g, SUBS_PER_PAGE, stride=SUB_ROWS), :]
                                 for g in range(G) for pg in pages], axis=0)
            lo = (x + pos_ref[j:j + 1, :]).astype(BF16)
            hi = (x + pos_ref[CMP_STRIDE + j:CMP_STRIDE + j + 1, :]).astype(BF16)
            first = first + jnp.dot(lo, w1_ref[j * NSA_DIM:(j + 1) * NSA_DIM, :], preferred_element_type=F32)
            second = second + jnp.dot(hi, w1_ref[(CMP_STRIDE + j) * NSA_DIM:(CMP_STRIDE + j + 1) * NSA_DIM, :],
                                      preferred_element_type=F32)
        h = first + pltpu.roll(second, G * N_SUB - 1, axis=0)
        out_ref[0] = jnp.dot(_gelu_tanh(h).astype(BF16), w2_ref[...], preferred_element_type=F32)

    compress(k_pages, pos_k, w1_k, w2_k, ok_ref)
    compress(v_pages, pos_v, w1_v, w2_v, ov_ref)


def _paged_specs(n):
    return [pl.BlockSpec((1, PAGE_ROWS, NSA_DIM), lambda b, pt, p=p: (pt[b, p], 0, 0)) for p in range(N_PAGES)] * n


def sample_compress(page_table, cmp_k, cmp_v, pos_k, w1_k, w2_k, pos_v, w1_v, w2_v):
    db = page_table.shape[0]
    full = lambda shape: pl.BlockSpec(shape, lambda b, pt: (0,) * len(shape))
    out = pl.BlockSpec((1, NSA_KV_HEADS * N_SUB, NSA_DIM), lambda b, pt: (b, 0, 0))
    return pl.pallas_call(
        _sample_compress_kernel,
        grid_spec=pltpu.PrefetchScalarGridSpec(
            num_scalar_prefetch=1, grid=(db,),
            in_specs=_paged_specs(1) + _paged_specs(1)
            + [full((CMP_LEN, NSA_DIM)), full((CMP_LEN * NSA_DIM, CMP_HIDDEN)), full((CMP_HIDDEN, NSA_DIM))] * 2,
            out_specs=[out, out]),
        out_shape=[jax.ShapeDtypeStruct((db, NSA_KV_HEADS * N_SUB, NSA_DIM), F32)] * 2,
        compiler_params=pltpu.CompilerParams(dimension_semantics=("parallel",), vmem_limit_bytes=VMEM_LIMIT),
        name="sample_compress",
    )(page_table, *([cmp_k] * N_PAGES), *([cmp_v] * N_PAGES), pos_k, w1_k.astype(BF16), w2_k.astype(BF16),
      pos_v, w1_v.astype(BF16), w2_v.astype(BF16))


def _sample_attention_kernel(pt_ref, *refs):
    k_pages, v_pages = refs[:N_PAGES], refs[N_PAGES:2 * N_PAGES]
    (wk_ref, wv_ref, kc_ref, vc_ref, q_ref, qr_ref, ksn_ref, vsn_ref, kwn_ref, vwn_ref, m_ref,
     o_ref, s_scr) = refs[2 * N_PAGES:]
    G, R = NSA_KV_HEADS, NSA_GROUP
    scale = NSA_DIM ** -0.5
    n_slc = -(-(PAST_LEN + DEC_SEQ) // SLC_BLOCK)
    blocks_per_page = PAGE_SIZE // SLC_BLOCK

    def iota(shape, axis):
        return lax.broadcasted_iota(jnp.int32, shape, axis)

    def q_side(width):
        r = iota((Q_ROWS, width), 0)
        return r // NSA_HEADS, (r % NSA_HEADS) // R

    def scores(q16, keys):
        return lax.dot_general(q16, keys.astype(BF16), NT_DIMS, preferred_element_type=F32) * scale

    def pad_new(ref):
        return jnp.concatenate([ref[...], jnp.zeros((LANES - NEW_ROWS, NSA_DIM), F32)], axis=0).astype(BF16)

    q16, qr16 = q_ref[...].astype(BF16), qr_ref[...].astype(BF16)

    tq, gq = q_side(G * N_SUB)
    c = iota((Q_ROWS, G * N_SUB), 1)
    ok = (c // N_SUB == gq) & (c % N_SUB < N_SUB - 1)
    s = jnp.where(ok, scores(q16, kc_ref[0]), NEG)
    e = jnp.exp(s - jnp.max(s, axis=1, keepdims=True))
    p = jnp.where(ok, e / jnp.sum(e, axis=1, keepdims=True), 0.0)
    p16 = p.astype(BF16)
    o_ref[0] = jnp.dot(p16, vc_ref[0].astype(BF16), preferred_element_type=F32)

    imp_head = jnp.dot(p16, m_ref[...], preferred_element_type=F32)
    tq, gq = q_side(Q_ROWS)
    cq = iota((Q_ROWS, Q_ROWS), 1)
    same_group = jnp.where((cq // NSA_HEADS == tq) & ((cq % NSA_HEADS) // R == gq), 1.0, 0.0)
    imp = _dot_split(same_group, imp_head)
    tq, gq = q_side(LANES)
    blk = iota((Q_ROWS, LANES), 1)
    pos = PAST_LEN + tq
    cur = pos // SLC_BLOCK
    valid = (blk * SLC_BLOCK <= pos) & (blk < n_slc)
    forced = (blk == 0) | (blk == cur) | (blk == cur - 1)
    score = jnp.where(valid, jnp.where(forced, FORCE, imp), NEG)
    rank = jnp.zeros((Q_ROWS, LANES), F32)
    for i in range(n_slc):
        si = score[:, i:i + 1]
        rank = rank + jnp.where((si > score) | ((si == score) & (i < blk)), 1.0, 0.0)
    sel = jnp.where((rank < SLC_TOPK) & valid, 1.0, 0.0)

    def new_mask():
        tq, gq = q_side(LANES)
        c = iota((Q_ROWS, LANES), 1)
        return (c < NEW_ROWS) & (c % G == gq) & (c // G <= tq)

    tq, gq = q_side(PAGE_ROWS)
    c = iota((Q_ROWS, PAGE_ROWS), 1)
    own_head = c % G == gq
    first_block = c // G < SLC_BLOCK
    for pg in range(N_PAGES):
        picked = jnp.where(first_block, sel[:, blocks_per_page * pg:blocks_per_page * pg + 1],
                           sel[:, blocks_per_page * pg + 1:blocks_per_page * pg + 2])
        s_scr[:, pg * PAGE_ROWS:(pg + 1) * PAGE_ROWS] = jnp.where(own_head & (picked > 0.5),
                                                                   scores(qr16, k_pages[pg][0]), NEG)
    past_cols = N_PAGES * PAGE_ROWS
    new_block = PAST_LEN // SLC_BLOCK
    s_scr[:, past_cols:past_cols + LANES] = jnp.where(new_mask() & (sel[:, new_block:new_block + 1] > 0.5),
                                                      scores(qr16, pad_new(ksn_ref)), NEG)
    s = s_scr[...]
    e = jnp.exp(s - jnp.max(s, axis=1, keepdims=True))
    acc = jnp.dot(e[:, past_cols:].astype(BF16), pad_new(vsn_ref), preferred_element_type=F32)
    for pg in range(N_PAGES):
        acc = acc + jnp.dot(e[:, pg * PAGE_ROWS:(pg + 1) * PAGE_ROWS].astype(BF16), v_pages[pg][0].astype(BF16),
                            preferred_element_type=F32)
    o_ref[1] = acc / jnp.sum(e, axis=1, keepdims=True)

    wrows = WINDOW * G
    tq, gq = q_side(wrows)
    c = iota((Q_ROWS, wrows), 1)
    s_buf = jnp.where((c % G == gq) & (c // G > tq), scores(qr16, wk_ref[0]), NEG)
    s_new = jnp.where(new_mask(), scores(qr16, pad_new(kwn_ref)), NEG)
    m = jnp.maximum(jnp.max(s_buf, axis=1, keepdims=True), jnp.max(s_new, axis=1, keepdims=True))
    e_buf, e_new = jnp.exp(s_buf - m), jnp.exp(s_new - m)
    acc = (jnp.dot(e_buf.astype(BF16), wv_ref[0].astype(BF16), preferred_element_type=F32)
           + jnp.dot(e_new.astype(BF16), pad_new(vwn_ref), preferred_element_type=F32))
    o_ref[2] = acc / (jnp.sum(e_buf, axis=1, keepdims=True) + jnp.sum(e_new, axis=1, keepdims=True))


def sample_attention(page_table, slc_k, slc_v, win_k, win_v, k_cmp, v_cmp, q, q_rot, ks_new, vs_new, kw_new, vw_new):
    db = page_table.shape[0]
    assert WINDOW == win_k.shape[1] // NSA_KV_HEADS and q.shape[0] == db * Q_ROWS
    n_cmp = N_SUB - 1
    n_slc = -(-(PAST_LEN + DEC_SEQ) // SLC_BLOCK)
    sel_m = np.zeros((NSA_KV_HEADS, N_SUB, LANES), np.float32)
    sel_m[:, :n_cmp, :n_slc] = selection_weights(n_cmp, n_slc)[None]
    sel_m = jnp.asarray(sel_m.reshape(NSA_KV_HEADS * N_SUB, LANES), BF16)
    per_seq = lambda rows: pl.BlockSpec((1, rows, NSA_DIM), lambda b, pt: (b, 0, 0))
    flat = lambda rows: pl.BlockSpec((rows, NSA_DIM), lambda b, pt: (b, 0))
    return pl.pallas_call(
        _sample_attention_kernel,
        grid_spec=pltpu.PrefetchScalarGridSpec(
            num_scalar_prefetch=1, grid=(db,),
            in_specs=_paged_specs(1) + _paged_specs(1)
            + [per_seq(WINDOW * NSA_KV_HEADS)] * 2 + [per_seq(NSA_KV_HEADS * N_SUB)] * 2
            + [flat(Q_ROWS)] * 2 + [flat(NEW_ROWS)] * 4
            + [pl.BlockSpec((NSA_KV_HEADS * N_SUB, LANES), lambda b, pt: (0, 0))],
            out_specs=pl.BlockSpec((3, Q_ROWS, NSA_DIM), lambda b, pt: (0, b, 0)),
            scratch_shapes=[pltpu.VMEM((Q_ROWS, N_PAGES * PAGE_ROWS + LANES), F32)]),
        out_shape=jax.ShapeDtypeStruct((3, db * Q_ROWS, NSA_DIM), F32),
        compiler_params=pltpu.CompilerParams(dimension_semantics=("parallel",), vmem_limit_bytes=VMEM_LIMIT),
        name="sample_attention",
    )(page_table, *([slc_k] * N_PAGES), *([slc_v] * N_PAGES), win_k, win_v, k_cmp, v_cmp, q, q_rot,
      ks_new, vs_new, kw_new, vw_new, sel_m)


PROMPT_TQ = 128


def _dot_split_rhs(a16, b):
    b_hi = b.astype(BF16)
    b_lo = (b - b_hi.astype(F32)).astype(BF16)
    return jnp.dot(a16, b_hi, preferred_element_type=F32) + jnp.dot(a16, b_lo, preferred_element_type=F32)


def _prompt_attention_kernel(q_ref, qr_ref, kc_ref, vc_ref, ks_ref, vs_ref, kw_ref, vw_ref, m_ref, x_ref, o_ref):
    R, TQ = NSA_GROUP, PROMPT_TQ
    QR = TQ * R
    T = ks_ref.shape[2]
    n_cmp = T // CMP_STRIDE - 1
    n_slc = -(-T // SLC_BLOCK)
    scale = NSA_DIM ** -0.5
    i = pl.program_id(2)

    def iota(shape, axis):
        return lax.broadcasted_iota(jnp.int32, shape, axis)

    def q_pos(width):
        return i * TQ + iota((QR, width), 0) // R

    def scores(q16, keys):
        return lax.dot_general(q16, keys.astype(BF16), NT_DIMS, preferred_element_type=F32) * scale

    q16, qr16 = q_ref[0, 0].astype(BF16), qr_ref[0, 0].astype(BF16)

    n = iota((QR, LANES), 1)
    pos = q_pos(LANES)
    ok = (n * CMP_STRIDE + (CMP_LEN - 1) <= pos) & (n < n_cmp)
    s = jnp.where(ok, scores(q16, kc_ref[0, 0]), NEG)
    e = jnp.exp(s - jnp.max(s, axis=1, keepdims=True))
    p16 = jnp.where(ok, e / jnp.sum(e, axis=1, keepdims=True), 0.0).astype(BF16)
    o_ref[0, 0, 0] = jnp.dot(p16, vc_ref[0, 0].astype(BF16), preferred_element_type=F32)

    imp_head = jnp.dot(p16, m_ref[...], preferred_element_type=F32)
    rows_of_token = jnp.where(iota((TQ, QR), 1) // R == iota((TQ, QR), 0), 1.0, 0.0).astype(BF16)
    imp = _dot_split_rhs(rows_of_token, imp_head)
    blk = iota((TQ, LANES), 1)
    pos = i * TQ + iota((TQ, LANES), 0)
    cur = pos // SLC_BLOCK
    valid = (blk * SLC_BLOCK <= pos) & (blk < n_slc)
    forced = (blk == 0) | (blk == cur) | (blk == cur - 1)
    score = jnp.where(valid, jnp.where(forced, FORCE, imp), NEG)
    rank = jnp.zeros((TQ, LANES), F32)
    for b in range(n_slc):
        sb = score[:, b:b + 1]
        rank = rank + jnp.where((sb > score) | ((sb == score) & (b < blk)), 1.0, 0.0)
    sel16 = jnp.where((rank < min(SLC_TOPK, n_slc)) & valid, 1.0, 0.0).astype(BF16)

    key_sel_tok = jnp.dot(sel16, x_ref[...], preferred_element_type=F32).astype(BF16)
    token_of_row = jnp.where(iota((QR, TQ), 0) // R == iota((QR, TQ), 1), 1.0, 0.0).astype(BF16)
    key_sel = jnp.dot(token_of_row, key_sel_tok, preferred_element_type=F32)
    ok = (key_sel > 0.5) & (iota((QR, T), 1) <= q_pos(T))
    s = jnp.where(ok, scores(qr16, ks_ref[0, 0]), NEG)
    e = jnp.exp(s - jnp.max(s, axis=1, keepdims=True))
    o_ref[1, 0, 0] = (jnp.dot(e.astype(BF16), vs_ref[0, 0].astype(BF16), preferred_element_type=F32)
                      / jnp.sum(e, axis=1, keepdims=True))

    span = WINDOW + TQ
    start = pl.multiple_of(jnp.maximum(i - WINDOW // TQ, 0) * TQ, TQ)
    k_pos = start + iota((QR, span), 1)
    pos = q_pos(span)
    ok = (k_pos <= pos) & (k_pos > pos - WINDOW)
    s = jnp.where(ok, scores(qr16, kw_ref[0, 0, pl.ds(start, span), :]), NEG)
    e = jnp.exp(s - jnp.max(s, axis=1, keepdims=True))
    o_ref[2, 0, 0] = (jnp.dot(e.astype(BF16), vw_ref[0, 0, pl.ds(start, span), :].astype(BF16),
                              preferred_element_type=F32) / jnp.sum(e, axis=1, keepdims=True))


def prompt_attention(q, q_rot, k_cmp, v_cmp, ks, vs, kw, vw):
    B, T, H, d = q.shape
    G, R, TQ = NSA_KV_HEADS, NSA_GROUP, PROMPT_TQ
    n_cmp = k_cmp.shape[1]
    n_slc = -(-T // SLC_BLOCK)
    assert T % TQ == 0 and n_cmp == T // CMP_STRIDE - 1 and n_cmp <= LANES and n_slc <= LANES and T >= WINDOW + TQ
    by_head = lambda a: jnp.swapaxes(a, 1, 2)
    q_rows = lambda a: jnp.swapaxes(a.reshape(B, T, G, R, d), 1, 2).reshape(B, G, T * R, d)
    pad_cmp = lambda a: jnp.pad(by_head(a), ((0, 0), (0, 0), (0, LANES - n_cmp), (0, 0)))
    sel_m = np.zeros((LANES, LANES), np.float32)
    sel_m[:n_cmp, :n_slc] = selection_weights(n_cmp, n_slc)
    expand = (np.arange(T)[None, :] // SLC_BLOCK == np.arange(LANES)[:, None]).astype(np.float32)
    rows = pl.BlockSpec((1, 1, TQ * R, d), lambda b, g, i: (b, g, i, 0))
    seq = lambda n: pl.BlockSpec((1, 1, n, d), lambda b, g, i: (b, g, 0, 0))
    const = lambda shape: pl.BlockSpec(shape, lambda b, g, i: (0, 0))
    o = pl.pallas_call(
        _prompt_attention_kernel,
        grid=(B, G, T // TQ),
        in_specs=[rows, rows, seq(LANES), seq(LANES), seq(T), seq(T), seq(T), seq(T),
                  const((LANES, LANES)), const((LANES, T))],
        out_specs=pl.BlockSpec((3, 1, 1, TQ * R, d), lambda b, g, i: (0, b, g, i, 0)),
        out_shape=jax.ShapeDtypeStruct((3, B, G, T * R, d), F32),
        compiler_params=pltpu.CompilerParams(dimension_semantics=("parallel", "parallel", "parallel"),
                                             vmem_limit_bytes=VMEM_LIMIT),
        name="prompt_attention",
    )(q_rows(q), q_rows(q_rot), pad_cmp(k_cmp), pad_cmp(v_cmp), by_head(ks), by_head(vs), by_head(kw), by_head(vw),
      jnp.asarray(sel_m, BF16), jnp.asarray(expand, BF16))
    return jnp.swapaxes(o.reshape(3, B, G, T, R, d), 2, 3).reshape(3, B, T, H, d)


ROUTE_TOKENS = LANES


def _top_rows(s, row_id, k, payload=None):
    slot = lax.broadcasted_iota(jnp.int32, (k, s.shape[1]), 0)
    vals = jnp.zeros((k, s.shape[1]), F32)
    picks = jnp.zeros((k, s.shape[1]), F32)
    for t in range(k):
        m = jnp.max(s, axis=0, keepdims=True)
        i = jnp.min(jnp.where(s == m, row_id, jnp.inf), axis=0, keepdims=True)
        hit = row_id == i
        pick = i if payload is None else jnp.sum(jnp.where(hit, payload, 0.0), axis=0, keepdims=True)
        vals = jnp.where(slot == t, m, vals)
        picks = jnp.where(slot == t, pick, picks)
        s = jnp.where(hit, -jnp.inf, s)
    return vals, picks


def _peer_route_kernel(q_ref, keys_ref, gate_ref, e_scr, g_scr):
    T = ROUTE_TOKENS
    half = PEER_QDIM // 2
    key_id = lax.broadcasted_iota(jnp.int32, (PEER_KEYS, T), 0).astype(F32)
    lo = PEER_TOPK // 2
    cand_row = lax.broadcasted_iota(jnp.int32, (lo * PEER_TOPK + lo, T), 0)
    cand_id = jnp.where(cand_row < lo * PEER_TOPK, cand_row,
                        lo * PEER_TOPK + (cand_row - lo * PEER_TOPK) * PEER_TOPK).astype(F32)
    experts, gates = [], []
    for h in range(PEER_HEADS):
        sub = []
        for p in range(2):
            q = q_ref[:, (2 * h + p) * half:(2 * h + p + 1) * half].astype(BF16)
            s = lax.dot_general(keys_ref[2 * h + p], q, NT_DIMS, preferred_element_type=F32)
            sub.append(_top_rows(s, key_id, PEER_TOPK))
        (s1, i1), (s2, i2) = sub
        cand = jnp.concatenate([s1[k:k + 1] + s2 for k in range(lo)] + [s1[lo:] + s2[0:1]], axis=0)
        cand_e = jnp.concatenate([i1[k:k + 1] * float(PEER_KEYS) + i2 for k in range(lo)]
                                 + [i1[lo:] * float(PEER_KEYS) + i2[0:1]], axis=0)
        top_s, top_e = _top_rows(cand, cand_id, PEER_TOPK, payload=cand_e)
        ex = jnp.exp(top_s - top_s[0:1])
        experts.append(top_e)
        gates.append(ex / jnp.sum(ex, axis=0, keepdims=True))
    e_scr[...] = jnp.concatenate(experts, axis=0).T
    g_scr[...] = jnp.concatenate(gates, axis=0).T

    sub_id = lax.broadcasted_iota(jnp.int32, (PEER_KEYS, PEER_HEADS * PEER_TOPK), 0).astype(F32)

    def scatter_token(n, carry):
        e = e_scr[pl.ds(n, 1), :]
        a = jnp.floor(e * (1.0 / PEER_KEYS))
        b = e - a * float(PEER_KEYS)
        lhs = jnp.where(sub_id == a, g_scr[pl.ds(n, 1), :], 0.0)
        lhs_hi = lhs.astype(BF16)
        lhs_lo = (lhs - lhs_hi.astype(F32)).astype(BF16)
        rhs = jnp.where(sub_id == b, 1.0, 0.0).astype(BF16)
        gate_ref[n] = (lax.dot_general(lhs_hi, rhs, NT_DIMS, preferred_element_type=F32)
                       + lax.dot_general(lhs_lo, rhs, NT_DIMS, preferred_element_type=F32))
        return carry

    lax.fori_loop(0, T, scatter_token, 0, unroll=8)


def peer_routing(q, sub_keys):
    n = q.shape[0]
    assert n % ROUTE_TOKENS == 0
    keys = sub_keys.reshape(PEER_HEADS * 2, PEER_KEYS, PEER_QDIM // 2).astype(BF16)
    j = PEER_HEADS * PEER_TOPK
    return pl.pallas_call(
        _peer_route_kernel,
        grid=(n // ROUTE_TOKENS,),
        in_specs=[pl.BlockSpec((ROUTE_TOKENS, q.shape[1]), lambda i: (i, 0)),
                  pl.BlockSpec(keys.shape, lambda i: (0, 0, 0))],
        out_specs=pl.BlockSpec((ROUTE_TOKENS, PEER_KEYS, PEER_KEYS), lambda i: (i, 0, 0)),
        out_shape=jax.ShapeDtypeStruct((n, PEER_KEYS, PEER_KEYS), F32),
        scratch_shapes=[pltpu.VMEM((ROUTE_TOKENS, j), F32), pltpu.VMEM((ROUTE_TOKENS, j), F32)],
        compiler_params=pltpu.CompilerParams(dimension_semantics=("parallel",), vmem_limit_bytes=VMEM_LIMIT),
        name="peer_routing",
    )(q, keys)


def _permuted_in_weight(w_in):
    cuts = np.cumsum(IN_SIZES)[:-1].tolist()
    parts = jnp.split(w_in, cuts, axis=-1)
    wide = [parts[i] for i in (0, 1, 2, 3, 6, 7, 8, 9, 10, 11, 12)]
    narrow = [parts[i] for i in (4, 5, 13)]
    n_narrow = sum(p.shape[1] for p in narrow)
    pad = jnp.zeros((w_in.shape[0], LANES - n_narrow), w_in.dtype)
    return jnp.concatenate(wide, axis=-1).astype(BF16), jnp.concatenate(narrow + [pad], axis=-1).astype(BF16)


def _split_projection(wide, narrow, B, T):
    sizes = [IN_SIZES[i] for i in (0, 1, 2, 3, 6, 7, 8, 9, 10, 11, 12)]
    cuts = np.cumsum(sizes)[:-1].tolist()
    gq, gk, gv, gz, nq, kc, vc, ks, vs, kw, vw = [p.reshape(B, T, -1) for p in jnp.split(wide, cuts, axis=-1)]
    ga = narrow[:, :GDN_HEADS].reshape(B, T, -1)
    gb = narrow[:, GDN_HEADS:2 * GDN_HEADS].reshape(B, T, -1)
    ngate = narrow[:, 2 * GDN_HEADS:2 * GDN_HEADS + 3 * NSA_HEADS].reshape(B, T, -1)
    return gq, gk, gv, gz, ga, gb, nq, kc, vc, ks, vs, kw, vw, ngate


def _prompt_group(proj, B, T, pos, gdn, weights):
    (conv_w, a_log, dt_bias, norm_w, pos_k, w1_k, w2_k, pos_v, w1_v, w2_v) = weights
    H, G, hd = NSA_HEADS, NSA_KV_HEADS, NSA_DIM
    (gq, gk, gv, gz, ga, gb, nq, kc, vc, ks, vs, kw, vw, ngate) = proj
    o_gdn, conv_new, ssm_new = gdn(gq, gk, gv, gz, ga, gb)
    q = nq.reshape(B, T, H, hd)
    q_rot = partial_rope(q, pos)
    kc = kc.reshape(B, T, G, hd)
    vc = vc.reshape(B, T, G, hd)
    ks = partial_rope(ks.reshape(B, T, G, hd), pos)
    vs = vs.reshape(B, T, G, hd)
    kw = partial_rope(kw.reshape(B, T, G, hd), pos)
    vw = vw.reshape(B, T, G, hd)
    k_cmp = compress_blocks(kc, pos_k, w1_k, w2_k)
    v_cmp = compress_blocks(vc, pos_v, w1_v, w2_v)
    o3 = prompt_attention(q, q_rot, k_cmp, v_cmp, ks, vs, kw, vw)
    keep = min(WINDOW, T)
    win_k_new, win_v_new = kw[:, T - keep:], vw[:, T - keep:]
    gate = jax.nn.sigmoid(ngate).reshape(B, T, 3, H, 1)
    o_nsa = gate[:, :, 0] * o3[0] + gate[:, :, 1] * o3[1] + gate[:, :, 2] * o3[2]
    mix_in = jnp.concatenate([o_gdn, o_nsa.reshape(B, T, NSA_QW)], axis=-1)
    return mix_in.reshape(B * T, D_MODEL), (kc, vc, ks, vs, win_k_new, win_v_new, conv_new, ssm_new)


def _sample_group(proj, DB, L, pos, page_table, pools, win_buf, gdn_state, weights):
    (conv_w, a_log, dt_bias, norm_w, pos_k, w1_k, w2_k, pos_v, w1_v, w2_v) = weights
    H, G, hd = NSA_HEADS, NSA_KV_HEADS, NSA_DIM
    assert L == DEC_SEQ and page_table.shape[1] == N_PAGES and pools[0].shape[1] == PAGE_SIZE
    (gq, gk, gv, gz, ga, gb, nq, kc, vc, ks, vs, kw, vw, ngate) = proj
    conv_buf, ssm0 = gdn_state
    o_gdn, conv_new, ssm_new = gdn_mixer(gq, gk, gv, gz, ga, gb, conv_buf, ssm0, conv_w, a_log, dt_bias, norm_w)
    q = nq.reshape(DB, L, H, hd)
    q_rot = partial_rope(q, pos)
    kc = kc.reshape(DB, L, G, hd)
    vc = vc.reshape(DB, L, G, hd)
    ks = partial_rope(ks.reshape(DB, L, G, hd), pos)
    vs = vs.reshape(DB, L, G, hd)
    kw = partial_rope(kw.reshape(DB, L, G, hd), pos)
    vw = vw.reshape(DB, L, G, hd)
    rows = lambda pool: pool.reshape(pool.shape[0], PAGE_ROWS, hd)
    cmp_k, cmp_v, slc_k, slc_v = [rows(p) for p in pools]
    k_buf, v_buf = win_buf
    k_cmp, v_cmp = sample_compress(page_table, cmp_k, cmp_v, pos_k, w1_k, w2_k, pos_v, w1_v, w2_v)
    new_rows = lambda a: a.reshape(DB * L * G, hd)
    o3 = sample_attention(page_table, slc_k, slc_v, k_buf.reshape(DB, WINDOW * G, hd), v_buf.reshape(DB, WINDOW * G, hd),
                          k_cmp, v_cmp, q.reshape(DB * L * H, hd), q_rot.reshape(DB * L * H, hd),
                          new_rows(ks), new_rows(vs), new_rows(kw), new_rows(vw)).reshape(3, DB, L, H, hd)
    gate = jax.nn.sigmoid(ngate).reshape(DB, L, 3, H, 1)
    o_nsa = gate[:, :, 0] * o3[0] + gate[:, :, 1] * o3[1] + gate[:, :, 2] * o3[2]
    win_k_new = jnp.concatenate([k_buf, kw], axis=1)[:, L:]
    win_v_new = jnp.concatenate([v_buf, vw], axis=1)[:, L:]
    mix_in = jnp.concatenate([o_gdn, o_nsa.reshape(DB, L, NSA_QW)], axis=-1)
    return mix_in.reshape(DB * L, D_MODEL), (kc, vc, ks, vs, win_k_new, win_v_new, conv_new, ssm_new)


def kernel(x_prompt, x_sample, cache_cmp_k, cache_cmp_v, cache_slc_k, cache_slc_v, cache_win_k, cache_win_v,
           state_conv, state_ssm, page_table, w_in, gdn_conv_w, gdn_a_log, gdn_dt_bias, gdn_norm_w,
           cmp_pos_k, cmp_w1_k, cmp_w2_k, cmp_pos_v, cmp_w1_v, cmp_w2_v, w_out, ln1_g, ln1_b,
           peer_w_query, peer_sub_keys, peer_u, peer_v, ln2_g, ln2_b):
    assert w_in.shape[0] == DEPTH == 1
    G, hd = NSA_KV_HEADS, NSA_DIM
    B, T, _ = x_prompt.shape
    DB, L, _ = x_sample.shape
    NP, NS = B * T, DB * L
    pos_p = jnp.arange(T, dtype=jnp.int32)
    pos_s = PAST_LEN + jnp.arange(L, dtype=jnp.int32)
    l = 0
    x_all = jnp.concatenate([x_prompt.reshape(NP, D_MODEL), x_sample.reshape(NS, D_MODEL)], axis=0)

    w_wide, w_narrow = _permuted_in_weight(w_in[l])
    x_bf = x_all.astype(BF16)
    proj_wide = matmul(x_bf, w_wide, 512, 1024, "in_proj_wide")
    proj_narrow = matmul(x_bf, w_narrow, 512, LANES, "in_proj_narrow")

    weights = (gdn_conv_w[l], gdn_a_log[l], gdn_dt_bias[l], gdn_norm_w[l], cmp_pos_k[l], cmp_w1_k[l], cmp_w2_k[l],
               cmp_pos_v[l], cmp_w1_v[l], cmp_w2_v[l])
    def gdn_prompt_group(gq, gk, gv, gz, ga, gb):
        o, ssm = gdn_prompt(proj_wide, ga, gb, gdn_conv_w[l], gdn_a_log[l], gdn_dt_bias[l], gdn_norm_w[l], B, T)
        conv_new = jnp.concatenate([gq, gk, gv], axis=-1)[:, T - (GDN_CONV - 1):]
        return o.reshape(B, T, GDN_VW), conv_new, ssm

    mix_p, new_p = _prompt_group(_split_projection(proj_wide[:NP], proj_narrow[:NP], B, T), B, T, pos_p,
                                 gdn_prompt_group, weights)

    mix_s, new_s = _sample_group(_split_projection(proj_wide[NP:], proj_narrow[NP:], DB, L), DB, L, pos_s, page_table,
                                 (cache_cmp_k[l], cache_cmp_v[l], cache_slc_k[l], cache_slc_v[l]),
                                 (cache_win_k[l], cache_win_v[l]), (state_conv[l], state_ssm[l]), weights)

    mix_in = jnp.concatenate([mix_p, mix_s], axis=0).astype(BF16)
    x1, x1_bf = matmul_residual_ln(mix_in, w_out[l].astype(BF16), x_all, ln1_g[l], ln1_b[l], 256, 512, "out_proj_ln")
    pq = matmul(x1_bf, peer_w_query[l].astype(BF16), 512, 1024, "peer_query")
    gate = peer_routing(pq, peer_sub_keys[l])
    y = peer_experts(x1_bf, peer_u[l].astype(BF16), peer_v[l].astype(BF16), gate, 512, 1024)
    x2 = layer_norm(DN_ALPHA * x1 + y, ln2_g[l], ln2_b[l])

    outs_p = [a[None] for a in new_p]
    outs_s = [a[None] for a in new_s]
    return (x2[:NP].reshape(B, T, D_MODEL), x2[NP:].reshape(DB, L, D_MODEL), *outs_p, *outs_s)
```
